```python
import jax, jax.numpy as jnp
from jax import lax
import numpy as np

D_MODEL = 1024
BATCH = 2
SEQ = 8192
DEPTH = 1

N_META = 16
BLOCK = 128
SB_HEADS = 8
SB_HEAD_DIM = 64
SB_WIDTH = SB_HEADS * SB_HEAD_DIM
HG_HEADS = 4
HG_EXPAND = 128
HG_HEAD_DIM = 128
HG_F_WIDTH = HG_HEADS * HG_EXPAND
HG_WIDTH = HG_HEADS * HG_HEAD_DIM
MIX_WIDTH = SB_WIDTH + HG_WIDTH
IN_WIDTH = 3 * SB_WIDTH + 2 * HG_F_WIDTH + 2 * HG_WIDTH
D_FF = -(-8 * D_MODEL // (3 * 256)) * 256
RMS_EPS = 1e-6

kernel_name = "hymba_stickbreak_hgrn2_hybrid"


def rmsnorm(x, g):
    xf = x.astype(jnp.float32)
    y = xf * lax.rsqrt(jnp.mean(xf * xf, axis=-1, keepdims=True) + RMS_EPS)
    return (y * g.astype(jnp.float32)).astype(x.dtype)


def group_rmsnorm(o, g, n_heads):
    B, L, W = o.shape
    of = o.astype(jnp.float32).reshape(B, L, n_heads, W // n_heads)
    of = of * lax.rsqrt(jnp.mean(of * of, axis=-1, keepdims=True) + RMS_EPS)
    return of.reshape(B, L, W) * g.astype(jnp.float32)


def split_heads(a, n_heads):
    B, L, W = a.shape
    return a.reshape(B, L, n_heads, W // n_heads).transpose(0, 2, 1, 3)


def merge_heads(a):
    B, H, L, d = a.shape
    return a.transpose(0, 2, 1, 3).reshape(B, L, H * d)


def stick_breaking_attention(q, k, v, key_valid):
    B, H, L, Dh = q.shape
    nb = L // BLOCK
    scale = Dh ** -0.5
    q_blocks = q.reshape(B, H, nb, BLOCK, Dh).transpose(2, 0, 1, 3, 4)
    key_pos = jnp.arange(L)

    def one_block(args):
        blk, qb = args
        q_pos = blk * BLOCK + jnp.arange(BLOCK)
        z = jnp.einsum('bhqd,bhkd->bhqk', qb, k).astype(jnp.float32) * scale
        visible = (key_pos[None, :] < q_pos[:, None]) & key_valid[None, :]
        log_beta = jax.nn.log_sigmoid(z)
        log_one_minus = jnp.where(visible, jax.nn.log_sigmoid(-z), 0.0)
        later = lax.cumsum(log_one_minus, axis=3, reverse=True) - log_one_minus
        w = jnp.where(visible, jnp.exp(log_beta + later), 0.0)
        return jnp.einsum('bhqk,bhkd->bhqd', w.astype(v.dtype), v)

    out = lax.map(one_block, (jnp.arange(nb), q_blocks))
    return out.transpose(1, 2, 0, 3, 4).reshape(B, H, L, Dh)


def hgrn2_chunkwise(q, k, v, log_f):
    B, H, L, Dk = q.shape
    Dv = v.shape[-1]
    nc = L // BLOCK

    def to_chunks(a):
        return a.astype(jnp.float32).reshape(B, H, nc, BLOCK, a.shape[-1]).transpose(2, 0, 1, 3, 4)

    causal = jnp.tril(jnp.ones((BLOCK, BLOCK), dtype=bool))

    def step(S, inp):
        qc, kc, vc, gc = inp
        b = jnp.cumsum(gc, axis=2)
        diff = b[:, :, :, None, :] - b[:, :, None, :, :]
        decay = jnp.exp(jnp.where(causal[:, :, None], diff, -jnp.inf))
        scores = jnp.einsum('bhtsk,bhsk->bhts', qc[:, :, :, None, :] * decay, kc)
        o = (jnp.einsum('bhts,bhsv->bhtv', scores, vc)
             + jnp.einsum('bhtk,bhkv->bhtv', qc * jnp.exp(b), S))
        b_last = b[:, :, -1:, :]
        S = (S * jnp.exp(b_last[:, :, 0, :, None])
             + jnp.einsum('bhsk,bhsv->bhkv', kc * jnp.exp(b_last - b), vc))
        return S, o

    S0 = jnp.zeros((B, H, Dk, Dv), jnp.float32)
    _, o = lax.scan(step, S0, (to_chunks(q), to_chunks(k), to_chunks(v), to_chunks(log_f)))
    return o.transpose(1, 2, 0, 3, 4).reshape(B, H, L, Dv)


def hybrid_layer(h, key_valid, lower_bound, norm1_g, w_in, sb_norm_g, hg_norm_g, w_out,
                 norm2_g, w_gate, w_up, w_down):
    dt = h.dtype
    u = rmsnorm(h, norm1_g)
    proj = u @ w_in
    cuts = np.cumsum([SB_WIDTH, SB_WIDTH, SB_WIDTH, HG_F_WIDTH, HG_F_WIDTH, HG_WIDTH])
    sb_q, sb_k, sb_v, hg_q, hg_f, hg_i, hg_g = jnp.split(proj, cuts, axis=-1)

    o_sb = stick_breaking_attention(split_heads(sb_q, SB_HEADS), split_heads(sb_k, SB_HEADS),
                                    split_heads(sb_v, SB_HEADS), key_valid)
    o_sb = group_rmsnorm(merge_heads(o_sb), sb_norm_g, SB_HEADS)

    lb = lower_bound.astype(jnp.float32)
    log_f = jnp.logaddexp(jnp.log(lb), jnp.log1p(-lb) + jax.nn.log_sigmoid(hg_f.astype(jnp.float32)))
    hg_k = -jnp.expm1(log_f)
    valid = key_valid[None, :, None]
    log_f = jnp.where(valid, log_f, 0.0)
    hg_k = jnp.where(valid, hg_k, 0.0)
    o_hg = hgrn2_chunkwise(split_heads(hg_q, HG_HEADS), split_heads(hg_k, HG_HEADS),
                           split_heads(hg_i, HG_HEADS), split_heads(log_f, HG_HEADS))
    o_hg = group_rmsnorm(merge_heads(o_hg), hg_norm_g, HG_HEADS) * jax.nn.silu(hg_g.astype(jnp.float32))

    mix = jnp.concatenate([o_sb, o_hg], axis=-1).astype(dt) @ w_out
    h = h + mix

    u2 = rmsnorm(h, norm2_g)
    ffn = (jax.nn.silu(u2 @ w_gate) * (u2 @ w_up)) @ w_down
    return h + ffn


def setup_inputs(seed: int = 0) -> dict:
    key = jax.random.key(seed)
    ks = jax.random.split(key, 16)
    f32 = jnp.float32
    nrm = lambda k, shape, s: jax.random.normal(k, shape, f32) * s
    return {
        "x": nrm(ks[0], (BATCH, SEQ, D_MODEL), 1.0),
        "meta_tokens": nrm(ks[1], (N_META, D_MODEL), 1.0),
        "norm1_g": 1.0 + nrm(ks[2], (DEPTH, D_MODEL), 0.02),
        "w_in": nrm(ks[3], (DEPTH, D_MODEL, IN_WIDTH), D_MODEL ** -0.5),
        "sb_norm_g": 1.0 + nrm(ks[4], (DEPTH, SB_WIDTH), 0.02),
        "hg_norm_g": 1.0 + nrm(ks[5], (DEPTH, HG_WIDTH), 0.02),
        "hg_lb_logits": nrm(ks[6], (DEPTH + 1, HG_F_WIDTH), 0.5),
        "w_out": nrm(ks[7], (DEPTH, MIX_WIDTH, D_MODEL), MIX_WIDTH ** -0.5),
        "norm2_g": 1.0 + nrm(ks[8], (DEPTH, D_MODEL), 0.02),
        "w_gate": nrm(ks[9], (DEPTH, D_MODEL, D_FF), D_MODEL ** -0.5),
        "w_up": nrm(ks[10], (DEPTH, D_MODEL, D_FF), D_MODEL ** -0.5),
        "w_down": nrm(ks[11], (DEPTH, D_FF, D_MODEL), D_FF ** -0.5),
        "final_norm_g": 1.0 + nrm(ks[12], (D_MODEL,), 0.02),
    }


def reference(x, meta_tokens, norm1_g, w_in, sb_norm_g, hg_norm_g, hg_lb_logits, w_out,
              norm2_g, w_gate, w_up, w_down, final_norm_g):
    B = x.shape[0]
    pad = jnp.zeros((B, BLOCK - N_META, D_MODEL), x.dtype)
    meta = jnp.broadcast_to(meta_tokens.astype(x.dtype)[None], (B, N_META, D_MODEL))
    h = jnp.concatenate([pad, meta, x], axis=1)
    L = h.shape[1]
    key_valid = jnp.arange(L) >= (BLOCK - N_META)
    lower_bounds = jnp.cumsum(jax.nn.softmax(hg_lb_logits.astype(jnp.float32), axis=0), axis=0)
    for layer in range(DEPTH):
        h = hybrid_layer(h, key_valid, lower_bounds[layer], norm1_g[layer], w_in[layer],
                         sb_norm_g[layer], hg_norm_g[layer], w_out[layer], norm2_g[layer],
                         w_gate[layer], w_up[layer], w_down[layer])
    h = rmsnorm(h, final_norm_g)
    return h[:, BLOCK:, :]
```

```python
import functools

import jax
import jax.numpy as jnp
import numpy as np
from jax import lax
from jax.experimental import pallas as pl
from jax.experimental.pallas import tpu as pltpu

F32 = jnp.float32
BF16 = jnp.bfloat16

D_MODEL = 1024
N_META = 16
SB_HEADS = 8
SB_HEAD_DIM = 64
SB_WIDTH = SB_HEADS * SB_HEAD_DIM
HG_HEADS = 4
HG_DIM = 128
HG_WIDTH = HG_HEADS * HG_DIM
IN_WIDTH = 3 * SB_WIDTH + 4 * HG_WIDTH
D_FF = 2816
RMS_EPS = 1e-6

LANES = 128
FRONT = 256
FIRST_VALID = FRONT - N_META
ATT_TILE = 256
HG_CHUNK = 128
HG_LEVELS = 7
PROJ_ROWS = 256
FFN_ROWS = 512
FFN_SPLIT = 2
VMEM_LIMIT = 56 * 1024 * 1024

NT_DIMS = (((1,), (1,)), ((), ()))
TN_DIMS = (((0,), (0,)), ((), ()))


def _const_spec(shape):
    return pl.BlockSpec(shape, lambda *_: (0,) * len(shape))


def _inproj_kernel(x_ref, head_ref, g_ref, w_ref, qkv_ref, hq_ref, hf_ref, hi_ref, hg_ref):
    j = pl.program_id(1)
    h = jnp.where(j == 0, head_ref[...], x_ref[0])
    u = h * lax.rsqrt(jnp.mean(h * h, axis=-1, keepdims=True) + RMS_EPS) * g_ref[...]
    ub = u.astype(BF16)

    def proj(lo, width):
        return jnp.dot(ub, w_ref[:, lo:lo + width], preferred_element_type=F32)

    qkv_ref[0, :, 0:SB_WIDTH] = (proj(0, SB_WIDTH) * (SB_HEAD_DIM ** -0.5)).astype(BF16)
    qkv_ref[0, :, SB_WIDTH:3 * SB_WIDTH] = proj(SB_WIDTH, 2 * SB_WIDTH).astype(BF16)
    base = 3 * SB_WIDTH
    hq_ref[0] = proj(base, HG_WIDTH).astype(BF16)
    hf_ref[0] = proj(base + HG_WIDTH, HG_WIDTH)
    hi_ref[0] = proj(base + 2 * HG_WIDTH, HG_WIDTH).astype(BF16)
    hg_ref[0] = proj(base + 3 * HG_WIDTH, HG_WIDTH)


def _input_projection(x, head, norm_g, w_in_bf16):
    batch, seq, _ = x.shape
    n_tiles = (FRONT + seq) // PROJ_ROWS
    rows = FRONT + seq
    row_spec = lambda width: pl.BlockSpec((1, PROJ_ROWS, width), lambda b, j: (b, j, 0))
    return pl.pallas_call(
        _inproj_kernel,
        grid=(batch, n_tiles),
        in_specs=[
            pl.BlockSpec((1, PROJ_ROWS, D_MODEL), lambda b, j: (b, jnp.maximum(j - 1, 0), 0)),
            _const_spec((PROJ_ROWS, D_MODEL)),
            _const_spec((1, D_MODEL)),
            _const_spec((D_MODEL, IN_WIDTH)),
        ],
        out_specs=[row_spec(3 * SB_WIDTH), row_spec(HG_WIDTH), row_spec(HG_WIDTH),
                   row_spec(HG_WIDTH), row_spec(HG_WIDTH)],
        out_shape=[
            jax.ShapeDtypeStruct((batch, rows, 3 * SB_WIDTH), BF16),
            jax.ShapeDtypeStruct((batch, rows, HG_WIDTH), BF16),
            jax.ShapeDtypeStruct((batch, rows, HG_WIDTH), F32),
            jax.ShapeDtypeStruct((batch, rows, HG_WIDTH), BF16),
            jax.ShapeDtypeStruct((batch, rows, HG_WIDTH), F32),
        ],
        compiler_params=pltpu.CompilerParams(
            dimension_semantics=("parallel", "parallel"), vmem_limit_bytes=VMEM_LIMIT),
        name="in_projection",
    )(x, head, norm_g, w_in_bf16)


def _attn_kernel(q_ref, k_ref, v_ref, tri_ref, g_ref, o_ref):
    tile = ATT_TILE
    i = pl.program_id(2) + 1
    lane = lax.broadcasted_iota(jnp.int32, (1, LANES), 1)
    first_head = lane < SB_HEAD_DIM
    q_pair = q_ref[0]
    zeros = jnp.zeros_like(q_pair)
    q_heads = (jnp.where(first_head, q_pair, zeros), jnp.where(first_head, zeros, q_pair))
    tri2 = tri_ref[...]

    def step(c, carry, masked):
        accs, sums = carry
        off = pl.multiple_of(c * tile, tile)
        kc = k_ref[0, pl.ds(off, tile), :]
        vc = v_ref[0, pl.ds(off, tile), :]
        if masked:
            t_pos = i * tile + lax.broadcasted_iota(jnp.int32, (tile, 1), 0)
            s_pos = c * tile + lax.broadcasted_iota(jnp.int32, (1, tile), 1)
            visible = (s_pos < t_pos) & (s_pos >= FIRST_VALID)
        new_accs, new_sums = [], []
        for e in range(2):
            z = lax.dot_general(q_heads[e], kc, NT_DIMS, preferred_element_type=F32)
            sp = jnp.maximum(z, 0.0) + jnp.log(1.0 + jnp.exp(-jnp.abs(z)))
            if masked:
                sp = jnp.where(visible, sp, 0.0)
            hi = sp.astype(BF16)
            lo = (sp - hi.astype(F32)).astype(BF16)
            csum = jnp.dot(jnp.concatenate([hi, lo], axis=1), tri2,
                           preferred_element_type=F32) + sums[e]
            w = jnp.exp(z - csum)
            if masked:
                w = jnp.where(visible, w, 0.0)
            new_accs.append(accs[e] + jnp.dot(w.astype(BF16), vc, preferred_element_type=F32))
            new_sums.append(csum[:, 0:1])
        return tuple(new_accs), tuple(new_sums)

    init = ((jnp.zeros((tile, LANES), F32),) * 2, (jnp.zeros((tile, 1), F32),) * 2)
    carry = step(i, init, True)
    carry = lax.fori_loop(1, i, lambda n, cr: step(i - n, cr, False), carry)
    accs, _ = step(0, carry, True)

    o = jnp.where(first_head, accs[0], accs[1])
    sq = o * o
    ss_first = jnp.sum(jnp.where(first_head, sq, 0.0), axis=-1, keepdims=True)
    ss_second = jnp.sum(sq, axis=-1, keepdims=True) - ss_first
    mean_sq = jnp.where(first_head, ss_first, ss_second) * (1.0 / SB_HEAD_DIM)
    o_ref[0] = (o * lax.rsqrt(mean_sq + RMS_EPS) * g_ref[...]).astype(o_ref.dtype)


def _reverse_cumsum_matrix(n):
    j = np.arange(n)[:, None]
    s = np.arange(n)[None, :]
    tri = (j >= s).astype(np.float32)
    return jnp.asarray(np.concatenate([tri, tri], axis=0), BF16)


def _attention(qkv, sb_norm_g, seq):
    batch, rows, _ = qkv.shape
    pairs = SB_WIDTH // LANES
    n_q = seq // ATT_TILE
    return pl.pallas_call(
        _attn_kernel,
        grid=(batch, pairs, n_q),
        in_specs=[
            pl.BlockSpec((1, ATT_TILE, LANES), lambda b, p, i: (b, i + 1, p)),
            pl.BlockSpec((1, rows, LANES), lambda b, p, i: (b, 0, pairs + p)),
            pl.BlockSpec((1, rows, LANES), lambda b, p, i: (b, 0, 2 * pairs + p)),
            _const_spec((2 * ATT_TILE, ATT_TILE)),
            pl.BlockSpec((1, LANES), lambda b, p, i: (0, p)),
        ],
        out_specs=pl.BlockSpec((1, ATT_TILE, LANES), lambda b, p, i: (b, i, p)),
        out_shape=jax.ShapeDtypeStruct((batch, seq, SB_WIDTH), BF16),
        compiler_params=pltpu.CompilerParams(
            dimension_semantics=("parallel", "parallel", "parallel"), vmem_limit_bytes=VMEM_LIMIT),
        name="stickbreak_attention",
    )(qkv, qkv, qkv, _reverse_cumsum_matrix(ATT_TILE), sb_norm_g)


def _hgrn_exponent_matrix():
    c = HG_CHUNK
    t = np.arange(c)[:, None]
    j = np.arange(c)[None, :]
    mats = [(j <= t)]
    for lvl in range(HG_LEVELS):
        m = c >> (lvl + 1)
        blk = t // m
        odd = (blk % 2) == 1
        prefix = (j >= blk * m) & (j <= t)
        suffix = (j > t) & (j <= blk * m + m - 1)
        mats.append(np.where(odd, prefix, suffix))
    z = np.concatenate(mats, axis=0).astype(np.float32)
    return jnp.asarray(np.concatenate([z, z], axis=1), BF16)


def _hgrn_kernel(q_ref, f_ref, i_ref, gate_ref, lbl_ref, ng_ref, zmat_ref, o_ref, state_ref):
    c = pl.program_id(1)
    ch = HG_CHUNK

    @pl.when(c == 0)
    def _():
        state_ref[...] = jnp.zeros_like(state_ref)

    logits = lbl_ref[...]
    mx = jnp.max(logits, axis=0, keepdims=True)
    ex = jnp.exp(logits - mx)
    lb = ex[0:1] / jnp.sum(ex, axis=0, keepdims=True)

    f_raw = f_ref[0]
    sig = 1.0 / (1.0 + jnp.exp(-f_raw))
    log_f = jnp.log(lb + (1.0 - lb) * sig)
    kk = (1.0 - lb) * (1.0 - sig)
    row = lax.broadcasted_iota(jnp.int32, (ch, 1), 0)
    valid = (c > 0) | (row >= ch - N_META)
    log_f = jnp.where(valid, log_f, 0.0)
    kk = jnp.where(valid, kk, 0.0)

    hi = log_f.astype(BF16)
    lo = (log_f - hi.astype(F32)).astype(BF16)
    expo = jnp.dot(zmat_ref[...], jnp.concatenate([hi, lo], axis=0),
                   preferred_element_type=F32)
    b = expo[0:ch]
    q = q_ref[0].astype(F32)
    v = i_ref[0]

    t_idx = lax.broadcasted_iota(jnp.int32, (ch, ch), 0)
    s_idx = lax.broadcasted_iota(jnp.int32, (ch, ch), 1)
    q_lvl, k_lvl, masks = [q.astype(BF16)], [kk.astype(BF16)], [t_idx == s_idx]
    for lvl in range(HG_LEVELS):
        m = ch >> (lvl + 1)
        decay = jnp.exp(expo[(lvl + 1) * ch:(lvl + 2) * ch])
        odd = ((row // m) % 2) == 1
        q_lvl.append(jnp.where(odd, q * decay, 0.0).astype(BF16))
        k_lvl.append(jnp.where(odd, 0.0, kk * decay).astype(BF16))
        masks.append((t_idx // (2 * m)) == (s_idx // (2 * m)))

    b_last = b[ch - 1:ch]
    q_inter = (q * jnp.exp(b)).astype(BF16)
    k_state = (kk * jnp.exp(b_last - b)).astype(BF16)
    state_decay = jnp.exp(b_last)

    outs = []
    for h in range(HG_HEADS):
        sl = slice(h * HG_DIM, (h + 1) * HG_DIM)
        scores = jnp.zeros((ch, ch), F32)
        for ql, kl, mask in zip(q_lvl, k_lvl, masks):
            part = lax.dot_general(ql[:, sl], kl[:, sl], NT_DIMS, preferred_element_type=F32)
            scores = scores + jnp.where(mask, part, 0.0)
        st = state_ref[h]
        o_h = (jnp.dot(scores.astype(BF16), v[:, sl], preferred_element_type=F32)
               + lax.dot_general(q_inter[:, sl], st.astype(BF16), NT_DIMS,
                                 preferred_element_type=F32))
        state_ref[h] = (st * state_decay[:, sl]
                        + lax.dot_general(v[:, sl], k_state[:, sl], TN_DIMS,
                                          preferred_element_type=F32))
        o_h = o_h * lax.rsqrt(jnp.mean(o_h * o_h, axis=-1, keepdims=True) + RMS_EPS)
        outs.append(o_h)
    o = jnp.concatenate(outs, axis=1) * ng_ref[...]
    gate = gate_ref[0]
    o_ref[0] = (o * (gate / (1.0 + jnp.exp(-gate)))).astype(o_ref.dtype)


def _hgrn(hq, hf, hi, hgate, lb_logits, hg_norm_g, seq):
    batch = hq.shape[0]
    first = FRONT // HG_CHUNK - 1
    n_chunks = seq // HG_CHUNK + 1
    in_spec = pl.BlockSpec((1, HG_CHUNK, HG_WIDTH), lambda b, c: (b, c + first, 0))
    zmat = _hgrn_exponent_matrix()
    return pl.pallas_call(
        _hgrn_kernel,
        grid=(batch, n_chunks),
        in_specs=[in_spec, in_spec, in_spec, in_spec,
                  _const_spec(lb_logits.shape), _const_spec((1, HG_WIDTH)),
                  _const_spec(zmat.shape)],
        out_specs=pl.BlockSpec((1, HG_CHUNK, HG_WIDTH), lambda b, c: (b, jnp.maximum(c - 1, 0), 0)),
        out_shape=jax.ShapeDtypeStruct((batch, seq, HG_WIDTH), BF16),
        scratch_shapes=[pltpu.VMEM((HG_HEADS, HG_DIM, HG_DIM), F32)],
        compiler_params=pltpu.CompilerParams(
            dimension_semantics=("parallel", "arbitrary"), vmem_limit_bytes=VMEM_LIMIT),
        name="hgrn2",
    )(hq, hf, hi, hgate, lb_logits, hg_norm_g, zmat)


def _ffn_kernel(x_ref, osb_ref, ohg_ref, wout_ref, g2_ref, wg_ref, wu_ref, wd_ref, gf_ref, o_ref):
    mix = (jnp.dot(osb_ref[0], wout_ref[0:SB_WIDTH, :], preferred_element_type=F32)
           + jnp.dot(ohg_ref[0], wout_ref[SB_WIDTH:, :], preferred_element_type=F32))
    h1 = x_ref[0] + mix
    u2 = (h1 * lax.rsqrt(jnp.mean(h1 * h1, axis=-1, keepdims=True) + RMS_EPS)
          * g2_ref[...]).astype(BF16)
    ffn = jnp.zeros_like(h1)
    width = D_FF // FFN_SPLIT
    for n in range(FFN_SPLIT):
        cols = slice(n * width, (n + 1) * width)
        gate = jnp.dot(u2, wg_ref[:, cols], preferred_element_type=F32)
        up = jnp.dot(u2, wu_ref[:, cols], preferred_element_type=F32)
        act = (gate / (1.0 + jnp.exp(-gate)) * up).astype(BF16)
        ffn = ffn + jnp.dot(act, wd_ref[cols, :], preferred_element_type=F32)
    h2 = h1 + ffn
    o_ref[0] = (h2 * lax.rsqrt(jnp.mean(h2 * h2, axis=-1, keepdims=True) + RMS_EPS)
                * gf_ref[...])


def _out_projection_ffn(x, o_sb, o_hg, w_out, norm2_g, w_gate, w_up, w_down, final_g):
    batch, seq, _ = x.shape
    row_spec = lambda width: pl.BlockSpec((1, FFN_ROWS, width), lambda b, j: (b, j, 0))
    resident = lambda shape: pl.BlockSpec(shape, lambda b, j: (0, 0), pipeline_mode=pl.Buffered(1))
    return pl.pallas_call(
        _ffn_kernel,
        grid=(batch, seq // FFN_ROWS),
        in_specs=[row_spec(D_MODEL), row_spec(SB_WIDTH), row_spec(HG_WIDTH),
                  resident((D_MODEL, D_MODEL)), resident((1, D_MODEL)),
                  resident((D_MODEL, D_FF)), resident((D_MODEL, D_FF)), resident((D_FF, D_MODEL)),
                  resident((1, D_MODEL))],
        out_specs=row_spec(D_MODEL),
        out_shape=jax.ShapeDtypeStruct((batch, seq, D_MODEL), x.dtype),
        compiler_params=pltpu.CompilerParams(
            dimension_semantics=("parallel", "parallel"), vmem_limit_bytes=VMEM_LIMIT),
        name="out_projection_ffn",
    )(x, o_sb, o_hg, w_out, norm2_g, w_gate, w_up, w_down, final_g)


def kernel(x, meta_tokens, norm1_g, w_in, sb_norm_g, hg_norm_g, hg_lb_logits, w_out, norm2_g,
           w_gate, w_up, w_down, final_norm_g):
    batch, seq, d_model = x.shape
    assert d_model == D_MODEL and seq % FFN_ROWS == 0 and seq % ATT_TILE == 0
    assert norm1_g.shape[0] == 1, "single-layer block"
    head = jnp.concatenate(
        [jnp.zeros((FIRST_VALID, D_MODEL), x.dtype), meta_tokens.astype(x.dtype)], axis=0)
    row = lambda g: g.reshape(1, -1).astype(F32)

    qkv, hq, hf, hi, hgate = _input_projection(x, head, row(norm1_g[0]), w_in[0].astype(BF16))
    o_sb = _attention(qkv, row(sb_norm_g[0]), seq)
    o_hg = _hgrn(hq, hf, hi, hgate, hg_lb_logits.astype(F32), row(hg_norm_g[0]), seq)
    return _out_projection_ffn(
        x, o_sb, o_hg, w_out[0].astype(BF16), row(norm2_g[0]), w_gate[0].astype(BF16),
        w_up[0].astype(BF16), w_down[0].astype(BF16), row(final_norm_g))
```

```python
import functools

import jax
import jax.numpy as jnp
import numpy as np
from jax import lax
from jax.experimental import pallas as pl
from jax.experimental.pallas import tpu as pltpu

F32 = jnp.float32
BF16 = jnp.bfloat16

D_MODEL = 1024
N_META = 16
SB_HEADS = 8
SB_HEAD_DIM = 64
SB_WIDTH = SB_HEADS * SB_HEAD_DIM
HG_HEADS = 4
HG_DIM = 128
HG_WIDTH = HG_HEADS * HG_DIM
IN_WIDTH = 3 * SB_WIDTH + 4 * HG_WIDTH
D_FF = 2816
RMS_EPS = 1e-6

LANES = 128
FRONT = 256
FIRST_VALID = FRONT - N_META
ATT_TILE = 256
HG_CHUNK = 128
HG_LEVELS = 7
PROJ_ROWS = 256
FFN_ROWS = 512
FFN_SPLIT = 2
VMEM_LIMIT = 56 * 1024 * 1024
Q_SCALE = float(np.log2(np.e)) * SB_HEAD_DIM ** -0.5
MASKED = -1e9

NT_DIMS = (((1,), (1,)), ((), ()))
TN_DIMS = (((0,), (0,)), ((), ()))


def _const_spec(shape):
    return pl.BlockSpec(shape, lambda *_: (0,) * len(shape))


def _inproj_kernel(x_ref, head_ref, g_ref, w_ref, qkv_ref, hq_ref, hf_ref, hi_ref, hg_ref):
    j = pl.program_id(1)
    h = jnp.where(j == 0, head_ref[...], x_ref[0])
    u = h * lax.rsqrt(jnp.mean(h * h, axis=-1, keepdims=True) + RMS_EPS) * g_ref[...]
    ub = u.astype(BF16)

    def proj(lo, width):
        return jnp.dot(ub, w_ref[:, lo:lo + width], preferred_element_type=F32)

    qkv_ref[0, :, 0:SB_WIDTH] = (proj(0, SB_WIDTH) * Q_SCALE).astype(BF16)
    qkv_ref[0, :, SB_WIDTH:3 * SB_WIDTH] = proj(SB_WIDTH, 2 * SB_WIDTH).astype(BF16)
    base = 3 * SB_WIDTH
    hq_ref[0] = proj(base, HG_WIDTH).astype(BF16)
    hf_ref[0] = proj(base + HG_WIDTH, HG_WIDTH)
    hi_ref[0] = proj(base + 2 * HG_WIDTH, HG_WIDTH).astype(BF16)
    hg_ref[0] = proj(base + 3 * HG_WIDTH, HG_WIDTH)


def _input_projection(x, head, norm_g, w_in_bf16):
    batch, seq, _ = x.shape
    n_tiles = (FRONT + seq) // PROJ_ROWS
    rows = FRONT + seq
    row_spec = lambda width: pl.BlockSpec((1, PROJ_ROWS, width), lambda b, j: (b, j, 0))
    return pl.pallas_call(
        _inproj_kernel,
        grid=(batch, n_tiles),
        in_specs=[
            pl.BlockSpec((1, PROJ_ROWS, D_MODEL), lambda b, j: (b, jnp.maximum(j - 1, 0), 0)),
            _const_spec((PROJ_ROWS, D_MODEL)),
            _const_spec((1, D_MODEL)),
            _const_spec((D_MODEL, IN_WIDTH)),
        ],
        out_specs=[row_spec(3 * SB_WIDTH), row_spec(HG_WIDTH), row_spec(HG_WIDTH),
                   row_spec(HG_WIDTH), row_spec(HG_WIDTH)],
        out_shape=[
            jax.ShapeDtypeStruct((batch, rows, 3 * SB_WIDTH), BF16),
            jax.ShapeDtypeStruct((batch, rows, HG_WIDTH), BF16),
            jax.ShapeDtypeStruct((batch, rows, HG_WIDTH), F32),
            jax.ShapeDtypeStruct((batch, rows, HG_WIDTH), BF16),
            jax.ShapeDtypeStruct((batch, rows, HG_WIDTH), F32),
        ],
        compiler_params=pltpu.CompilerParams(
            dimension_semantics=("parallel", "parallel"), vmem_limit_bytes=VMEM_LIMIT),
        name="in_projection",
    )(x, head, norm_g, w_in_bf16)


def _attn_kernel(q_ref, k_ref, v_ref, tri_ref, g_ref, o_ref,
                 z_ref, y_ref, c_ref, sum_ref, acc_ref):
    tile = ATT_TILE
    n_chunks = k_ref.shape[1] // tile
    i = pl.program_id(2) + 1
    lane = lax.broadcasted_iota(jnp.int32, (1, LANES), 1)
    first_head = lane < SB_HEAD_DIM
    q_pair = q_ref[0]
    zeros = jnp.zeros_like(q_pair)
    q_heads = (jnp.where(first_head, q_pair, zeros), jnp.where(first_head, zeros, q_pair))
    def chunk(ref, c):
        return ref[0, pl.ds(pl.multiple_of(c * tile, tile), tile), :]

    def step(n, masked):
        c_new = i - n
        kc = chunk(k_ref, jnp.maximum(c_new, 0))
        vc = chunk(v_ref, jnp.clip(c_new + 2, 0, n_chunks - 1))
        z_new = [lax.dot_general(q_heads[e], kc, NT_DIMS, preferred_element_type=F32)
                 for e in range(2)]
        if masked:
            t_pos = i * tile + lax.broadcasted_iota(jnp.int32, (tile, 1), 0)
            s_pos = c_new * tile + lax.broadcasted_iota(jnp.int32, (1, tile), 1)
            visible = (s_pos < t_pos) & (s_pos >= FIRST_VALID) & (c_new >= 0)
            z_new = [jnp.where(visible, z, MASKED) for z in z_new]
        later = []
        for e in range(2):
            local = c_ref[e]
            w = jnp.exp2(y_ref[e] - local).astype(BF16)
            acc_ref[e] += jnp.dot(w, vc, preferred_element_type=F32)
            later.append(sum_ref[e] + jnp.broadcast_to(local[:, 0:1], (tile, LANES)))
            sum_ref[e] = later[e]
        for e in range(2):
            z = z_ref[e]
            sp = jnp.maximum(z, 0.0) + jnp.log2(1.0 + jnp.exp2(-jnp.abs(z)))
            y_ref[e] = z - jnp.concatenate([later[e], later[e]], axis=1)
            c_ref[e] = jnp.dot(sp.astype(BF16), tri_ref[...], preferred_element_type=F32)
        for e in range(2):
            z_ref[e] = z_new[e]

    z_ref[...] = jnp.full(z_ref.shape, MASKED, F32)
    y_ref[...] = jnp.full(y_ref.shape, MASKED, F32)
    c_ref[...] = jnp.zeros(c_ref.shape, F32)
    sum_ref[...] = jnp.zeros(sum_ref.shape, F32)
    acc_ref[...] = jnp.zeros(acc_ref.shape, F32)
    step(0, True)

    @pl.loop(1, i)
    def _(n):
        step(n, False)

    @pl.loop(i, i + 3)
    def _(n):
        step(n, True)

    accs = (acc_ref[0], acc_ref[1])

    o = jnp.where(first_head, accs[0], accs[1])
    sq = o * o
    ss_first = jnp.sum(jnp.where(first_head, sq, 0.0), axis=-1, keepdims=True)
    ss_second = jnp.sum(sq, axis=-1, keepdims=True) - ss_first
    mean_sq = jnp.where(first_head, ss_first, ss_second) * (1.0 / SB_HEAD_DIM)
    o_ref[0] = (o * lax.rsqrt(mean_sq + RMS_EPS) * g_ref[...]).astype(o_ref.dtype)


def _reverse_cumsum_matrix(n):
    j = np.arange(n)[:, None]
    s = np.arange(n)[None, :]
    return jnp.asarray((j >= s).astype(np.float32), BF16)


def _attention(qkv, sb_norm_g, seq):
    batch, rows, _ = qkv.shape
    pairs = SB_WIDTH // LANES
    n_q = seq // ATT_TILE
    return pl.pallas_call(
        _attn_kernel,
        grid=(batch, pairs, n_q),
        in_specs=[
            pl.BlockSpec((1, ATT_TILE, LANES), lambda b, p, i: (b, i + 1, p)),
            pl.BlockSpec((1, rows, LANES), lambda b, p, i: (b, 0, pairs + p)),
            pl.BlockSpec((1, rows, LANES), lambda b, p, i: (b, 0, 2 * pairs + p)),
            _const_spec((ATT_TILE, ATT_TILE)),
            pl.BlockSpec((1, LANES), lambda b, p, i: (0, p)),
        ],
        out_specs=pl.BlockSpec((1, ATT_TILE, LANES), lambda b, p, i: (b, i, p)),
        out_shape=jax.ShapeDtypeStruct((batch, seq, SB_WIDTH), BF16),
        scratch_shapes=[pltpu.VMEM((2, ATT_TILE, ATT_TILE), F32),
                        pltpu.VMEM((2, ATT_TILE, ATT_TILE), F32),
                        pltpu.VMEM((2, ATT_TILE, ATT_TILE), F32),
                        pltpu.VMEM((2, ATT_TILE, LANES), F32),
                        pltpu.VMEM((2, ATT_TILE, LANES), F32)],
        compiler_params=pltpu.CompilerParams(
            dimension_semantics=("parallel", "parallel", "parallel"), vmem_limit_bytes=VMEM_LIMIT),
        name="stickbreak_attention",
    )(qkv, qkv, qkv, _reverse_cumsum_matrix(ATT_TILE), sb_norm_g)


def _hgrn_exponent_matrix():
    c = HG_CHUNK
    t = np.arange(c)[:, None]
    j = np.arange(c)[None, :]
    mats = [(j <= t)]
    for lvl in range(HG_LEVELS):
        m = c >> (lvl + 1)
        blk = t // m
        odd = (blk % 2) == 1
        prefix = (j >= blk * m) & (j <= t)
        suffix = (j > t) & (j <= blk * m + m - 1)
        mats.append(np.where(odd, prefix, suffix))
    z = np.concatenate(mats, axis=0).astype(np.float32)
    return jnp.asarray(np.concatenate([z, z], axis=1), BF16)


def _hgrn_kernel(q_ref, f_ref, i_ref, gate_ref, lbl_ref, ng_ref, zmat_ref, o_ref, state_ref):
    c = pl.program_id(1)
    ch = HG_CHUNK

    @pl.when(c == 0)
    def _():
        state_ref[...] = jnp.zeros_like(state_ref)

    logits = lbl_ref[...]
    mx = jnp.max(logits, axis=0, keepdims=True)
    ex = jnp.exp(logits - mx)
    lb = ex[0:1] / jnp.sum(ex, axis=0, keepdims=True)

    f_raw = f_ref[0]
    sig = 1.0 / (1.0 + jnp.exp(-f_raw))
    log_f = jnp.log(lb + (1.0 - lb) * sig)
    kk = (1.0 - lb) * (1.0 - sig)
    row = lax.broadcasted_iota(jnp.int32, (ch, 1), 0)
    valid = (c > 0) | (row >= ch - N_META)
    log_f = jnp.where(valid, log_f, 0.0)
    kk = jnp.where(valid, kk, 0.0)

    hi = log_f.astype(BF16)
    lo = (log_f - hi.astype(F32)).astype(BF16)
    expo = jnp.dot(zmat_ref[...], jnp.concatenate([hi, lo], axis=0),
                   preferred_element_type=F32)
    b = expo[0:ch]
    q = q_ref[0].astype(F32)
    v = i_ref[0]

    t_idx = lax.broadcasted_iota(jnp.int32, (ch, ch), 0)
    s_idx = lax.broadcasted_iota(jnp.int32, (ch, ch), 1)
    q_lvl, k_lvl, masks = [q.astype(BF16)], [kk.astype(BF16)], [t_idx == s_idx]
    for lvl in range(HG_LEVELS):
        m = ch >> (lvl + 1)
        decay = jnp.exp(expo[(lvl + 1) * ch:(lvl + 2) * ch])
        odd = ((row // m) % 2) == 1
        q_lvl.append(jnp.where(odd, q * decay, 0.0).astype(BF16))
        k_lvl.append(jnp.where(odd, 0.0, kk * decay).astype(BF16))
        masks.append((t_idx // (2 * m)) == (s_idx // (2 * m)))

    b_last = b[ch - 1:ch]
    q_inter = (q * jnp.exp(b)).astype(BF16)
    k_state = (kk * jnp.exp(b_last - b)).astype(BF16)
    state_decay = jnp.exp(b_last)

    outs = []
    for h in range(HG_HEADS):
        sl = slice(h * HG_DIM, (h + 1) * HG_DIM)
        scores = jnp.zeros((ch, ch), F32)
        for ql, kl, mask in zip(q_lvl, k_lvl, masks):
            part = lax.dot_general(ql[:, sl], kl[:, sl], NT_DIMS, preferred_element_type=F32)
            scores = scores + jnp.where(mask, part, 0.0)
        st = state_ref[h]
        o_h = (jnp.dot(scores.astype(BF16), v[:, sl], preferred_element_type=F32)
               + lax.dot_general(q_inter[:, sl], st.astype(BF16), NT_DIMS,
                                 preferred_element_type=F32))
        state_ref[h] = (st * state_decay[:, sl]
                        + lax.dot_general(v[:, sl], k_state[:, sl], TN_DIMS,
                                          preferred_element_type=F32))
        o_h = o_h * lax.rsqrt(jnp.mean(o_h * o_h, axis=-1, keepdims=True) + RMS_EPS)
        outs.append(o_h)
    o = jnp.concatenate(outs, axis=1) * ng_ref[...]
    gate = gate_ref[0]
    o_ref[0] = (o * (gate / (1.0 + jnp.exp(-gate)))).astype(o_ref.dtype)


def _hgrn(hq, hf, hi, hgate, lb_logits, hg_norm_g, seq):
    batch = hq.shape[0]
    first = FRONT // HG_CHUNK - 1
    n_chunks = seq // HG_CHUNK + 1
    in_spec = pl.BlockSpec((1, HG_CHUNK, HG_WIDTH), lambda b, c: (b, c + first, 0))
    zmat = _hgrn_exponent_matrix()
    return pl.pallas_call(
        _hgrn_kernel,
        grid=(batch, n_chunks),
        in_specs=[in_spec, in_spec, in_spec, in_spec,
                  _const_spec(lb_logits.shape), _const_spec((1, HG_WIDTH)),
                  _const_spec(zmat.shape)],
        out_specs=pl.BlockSpec((1, HG_CHUNK, HG_WIDTH), lambda b, c: (b, jnp.maximum(c - 1, 0), 0)),
        out_shape=jax.ShapeDtypeStruct((batch, seq, HG_WIDTH), BF16),
        scratch_shapes=[pltpu.VMEM((HG_HEADS, HG_DIM, HG_DIM), F32)],
        compiler_params=pltpu.CompilerParams(
            dimension_semantics=("parallel", "arbitrary"), vmem_limit_bytes=VMEM_LIMIT),
        name="hgrn2",
    )(hq, hf, hi, hgate, lb_logits, hg_norm_g, zmat)


def _ffn_kernel(x_ref, osb_ref, ohg_ref, wout_ref, g2_ref, wg_ref, wu_ref, wd_ref, gf_ref, o_ref):
    mix = (jnp.dot(osb_ref[0], wout_ref[0:SB_WIDTH, :], preferred_element_type=F32)
           + jnp.dot(ohg_ref[0], wout_ref[SB_WIDTH:, :], preferred_element_type=F32))
    h1 = x_ref[0] + mix
    u2 = (h1 * lax.rsqrt(jnp.mean(h1 * h1, axis=-1, keepdims=True) + RMS_EPS)
          * g2_ref[...]).astype(BF16)
    ffn = jnp.zeros_like(h1)
    width = D_FF // FFN_SPLIT
    for n in range(FFN_SPLIT):
        cols = slice(n * width, (n + 1) * width)
        gate = jnp.dot(u2, wg_ref[:, cols], preferred_element_type=F32)
        up = jnp.dot(u2, wu_ref[:, cols], preferred_element_type=F32)
        act = (gate / (1.0 + jnp.exp(-gate)) * up).astype(BF16)
        ffn = ffn + jnp.dot(act, wd_ref[cols, :], preferred_element_type=F32)
    h2 = h1 + ffn
    o_ref[0] = (h2 * lax.rsqrt(jnp.mean(h2 * h2, axis=-1, keepdims=True) + RMS_EPS)
                * gf_ref[...])


def _out_projection_ffn(x, o_sb, o_hg, w_out, norm2_g, w_gate, w_up, w_down, final_g):
    batch, seq, _ = x.shape
    row_spec = lambda width: pl.BlockSpec((1, FFN_ROWS, width), lambda b, j: (b, j, 0))
    resident = lambda shape: pl.BlockSpec(shape, lambda b, j: (0, 0), pipeline_mode=pl.Buffered(1))
    return pl.pallas_call(
        _ffn_kernel,
        grid=(batch, seq // FFN_ROWS),
        in_specs=[row_spec(D_MODEL), row_spec(SB_WIDTH), row_spec(HG_WIDTH),
                  resident((D_MODEL, D_MODEL)), resident((1, D_MODEL)),
                  resident((D_MODEL, D_FF)), resident((D_MODEL, D_FF)), resident((D_FF, D_MODEL)),
                  resident((1, D_MODEL))],
        out_specs=row_spec(D_MODEL),
        out_shape=jax.ShapeDtypeStruct((batch, seq, D_MODEL), x.dtype),
        compiler_params=pltpu.CompilerParams(
            dimension_semantics=("parallel", "parallel"), vmem_limit_bytes=VMEM_LIMIT),
        name="out_projection_ffn",
    )(x, o_sb, o_hg, w_out, norm2_g, w_gate, w_up, w_down, final_g)


def kernel(x, meta_tokens, norm1_g, w_in, sb_norm_g, hg_norm_g, hg_lb_logits, w_out, norm2_g,
           w_gate, w_up, w_down, final_norm_g):
    batch, seq, d_model = x.shape
    assert d_model == D_MODEL and seq % FFN_ROWS == 0 and seq % ATT_TILE == 0
    assert norm1_g.shape[0] == 1, "single-layer block"
    head = jnp.concatenate(
        [jnp.zeros((FIRST_VALID, D_MODEL), x.dtype), meta_tokens.astype(x.dtype)], axis=0)
    row = lambda g: g.reshape(1, -1).astype(F32)

    qkv, hq, hf, hi, hgate = _input_projection(x, head, row(norm1_g[0]), w_in[0].astype(BF16))
    o_sb = _attention(qkv, row(sb_norm_g[0]), seq)
    o_hg = _hgrn(hq, hf, hi, hgate, hg_lb_logits.astype(F32), row(hg_norm_g[0]), seq)
    return _out_projection_ffn(
        x, o_sb, o_hg, w_out[0].astype(BF16), row(norm2_g[0]), w_gate[0].astype(BF16),
        w_up[0].astype(BF16), w_down[0].astype(BF16), row(final_norm_g))
```

```python
import functools

import jax
import jax.numpy as jnp
import numpy as np
from jax import lax
from jax.experimental import pallas as pl
from jax.experimental.pallas import tpu as pltpu

F32 = jnp.float32
BF16 = jnp.bfloat16

D_MODEL = 1024
N_META = 16
SB_HEADS = 8
SB_HEAD_DIM = 64
SB_WIDTH = SB_HEADS * SB_HEAD_DIM
HG_HEADS = 4
HG_DIM = 128
HG_WIDTH = HG_HEADS * HG_DIM
IN_WIDTH = 3 * SB_WIDTH + 4 * HG_WIDTH
D_FF = 2816
RMS_EPS = 1e-6

LANES = 128
FRONT = 256
FIRST_VALID = FRONT - N_META
ATT_TILE = 256
ATT_PAIRS = 4
HG_CHUNK = 128
HG_LEVELS = 7
PROJ_ROWS = 256
FFN_ROWS = 512
FFN_SPLIT = 2
VMEM_LIMIT = 56 * 1024 * 1024
Q_SCALE = float(np.log2(np.e)) * SB_HEAD_DIM ** -0.5
MASKED = -1e9
SP_LINEAR = 64.0

NT_DIMS = (((1,), (1,)), ((), ()))
TN_DIMS = (((0,), (0,)), ((), ()))


def _const_spec(shape):
    return pl.BlockSpec(shape, lambda *_: (0,) * len(shape))


def _inproj_kernel(x_ref, head_ref, g_ref, w_ref, qkv_ref, hq_ref, hf_ref, hi_ref, hg_ref):
    j = pl.program_id(1)
    h = jnp.where(j == 0, head_ref[...], x_ref[0])
    u = h * lax.rsqrt(jnp.mean(h * h, axis=-1, keepdims=True) + RMS_EPS) * g_ref[...]
    ub = u.astype(BF16)

    def proj(lo, width):
        return jnp.dot(ub, w_ref[:, lo:lo + width], preferred_element_type=F32)

    qkv_ref[0, :, 0:SB_WIDTH] = (proj(0, SB_WIDTH) * Q_SCALE).astype(BF16)
    qkv_ref[0, :, SB_WIDTH:3 * SB_WIDTH] = proj(SB_WIDTH, 2 * SB_WIDTH).astype(BF16)
    base = 3 * SB_WIDTH
    hq_ref[0] = proj(base, HG_WIDTH).astype(BF16)
    hf_ref[0] = proj(base + HG_WIDTH, HG_WIDTH)
    hi_ref[0] = proj(base + 2 * HG_WIDTH, HG_WIDTH).astype(BF16)
    hg_ref[0] = proj(base + 3 * HG_WIDTH, HG_WIDTH)


def _input_projection(x, head, norm_g, w_in_bf16):
    batch, seq, _ = x.shape
    n_tiles = (FRONT + seq) // PROJ_ROWS
    rows = FRONT + seq
    row_spec = lambda width: pl.BlockSpec((1, PROJ_ROWS, width), lambda b, j: (b, j, 0))
    return pl.pallas_call(
        _inproj_kernel,
        grid=(batch, n_tiles),
        in_specs=[
            pl.BlockSpec((1, PROJ_ROWS, D_MODEL), lambda b, j: (b, jnp.maximum(j - 1, 0), 0)),
            _const_spec((PROJ_ROWS, D_MODEL)),
            _const_spec((1, D_MODEL)),
            _const_spec((D_MODEL, IN_WIDTH)),
        ],
        out_specs=[row_spec(3 * SB_WIDTH), row_spec(HG_WIDTH), row_spec(HG_WIDTH),
                   row_spec(HG_WIDTH), row_spec(HG_WIDTH)],
        out_shape=[
            jax.ShapeDtypeStruct((batch, rows, 3 * SB_WIDTH), BF16),
            jax.ShapeDtypeStruct((batch, rows, HG_WIDTH), BF16),
            jax.ShapeDtypeStruct((batch, rows, HG_WIDTH), F32),
            jax.ShapeDtypeStruct((batch, rows, HG_WIDTH), BF16),
            jax.ShapeDtypeStruct((batch, rows, HG_WIDTH), F32),
        ],
        compiler_params=pltpu.CompilerParams(
            dimension_semantics=("parallel", "parallel"), vmem_limit_bytes=VMEM_LIMIT),
        name="in_projection",
    )(x, head, norm_g, w_in_bf16)


def _attn_kernel(q_ref, k_ref, v_ref, tri_ref, g_ref, o_ref,
                 qh_ref, z_ref, y_ref, c_ref, sum_ref, acc_ref):
    tile = ATT_TILE
    n_chunks = k_ref.shape[1] // tile
    n_heads = z_ref.shape[0]
    i = pl.program_id(2) + 1
    lane = lax.broadcasted_iota(jnp.int32, (1, LANES), 1)
    first_head = lane < SB_HEAD_DIM

    def pair_lanes(h):
        return slice((h // 2) * LANES, (h // 2 + 1) * LANES)

    def chunk(ref, c, h):
        return ref[0, pl.ds(pl.multiple_of(c * tile, tile), tile), pair_lanes(h)]

    def step(n, masked, stages="ABC"):
        c_new = i - n
        c_a = jnp.maximum(c_new, 0)
        c_c = jnp.clip(c_new + 2, 0, n_chunks - 1)
        if masked:
            t_pos = i * tile + lax.broadcasted_iota(jnp.int32, (tile, 1), 0)
            s_pos = c_new * tile + lax.broadcasted_iota(jnp.int32, (1, tile), 1)
            visible = (s_pos < t_pos) & (s_pos >= FIRST_VALID)
        for h in range(n_heads):
            if "C" in stages:
                local = c_ref[h]
                w = jnp.exp2(y_ref[h] - local).astype(BF16)
                acc_ref[h] += jnp.dot(w, chunk(v_ref, c_c, h), preferred_element_type=F32)
                later = sum_ref[h] + jnp.broadcast_to(local[:, 0:1], (tile, LANES))
                sum_ref[h] = later
            if "B" in stages:
                z = z_ref[h]
            if "A" in stages:
                z_new = lax.dot_general(qh_ref[h], chunk(k_ref, c_a, h), NT_DIMS,
                                        preferred_element_type=F32)
                if masked:
                    z_new = jnp.where(visible, z_new, MASKED)
                z_ref[h] = z_new
            if "B" in stages:
                sp = jnp.maximum(z, jnp.log2(1.0 + jnp.exp2(jnp.minimum(z, SP_LINEAR))))
                y_ref[h] = z - jnp.concatenate([later, later], axis=1)
                c_ref[h] = jnp.dot(sp.astype(BF16), tri_ref[...], preferred_element_type=F32)

    for h in range(n_heads):
        q_pair = q_ref[0, :, pair_lanes(h)]
        qh_ref[h] = jnp.where(first_head == (h % 2 == 0), q_pair, jnp.zeros_like(q_pair))
    y_ref[...] = jnp.full(y_ref.shape, MASKED, F32)
    c_ref[...] = jnp.zeros(c_ref.shape, F32)
    sum_ref[...] = jnp.zeros(sum_ref.shape, F32)
    acc_ref[...] = jnp.zeros(acc_ref.shape, F32)
    step(0, True, "A")

    @pl.loop(1, i)
    def _(n):
        step(n, False)

    step(i, True)
    step(i + 1, False, "BC")
    step(i + 2, False, "C")

    for p in range(n_heads // 2):
        o = jnp.where(first_head, acc_ref[2 * p], acc_ref[2 * p + 1])
        sq = o * o
        ss_first = jnp.sum(jnp.where(first_head, sq, 0.0), axis=-1, keepdims=True)
        ss_second = jnp.sum(sq, axis=-1, keepdims=True) - ss_first
        mean_sq = jnp.where(first_head, ss_first, ss_second) * (1.0 / SB_HEAD_DIM)
        lanes = slice(p * LANES, (p + 1) * LANES)
        o_ref[0, :, lanes] = (o * lax.rsqrt(mean_sq + RMS_EPS) * g_ref[:, lanes]).astype(o_ref.dtype)


def _reverse_cumsum_matrix(n):
    j = np.arange(n)[:, None]
    s = np.arange(n)[None, :]
    return jnp.asarray((j >= s).astype(np.float32), BF16)


def _attention(qkv, sb_norm_g, seq):
    batch, rows, _ = qkv.shape
    width = ATT_PAIRS * LANES
    groups = SB_WIDTH // width
    n_q = seq // ATT_TILE
    heads = 2 * ATT_PAIRS
    whole_seq = lambda part: pl.BlockSpec(
        (1, rows, width), lambda b, g, i: (b, 0, part * groups + g), pipeline_mode=pl.Buffered(1))
    return pl.pallas_call(
        _attn_kernel,
        grid=(batch, groups, n_q),
        in_specs=[
            pl.BlockSpec((1, ATT_TILE, width), lambda b, g, i: (b, i + 1, g)),
            whole_seq(1),
            whole_seq(2),
            _const_spec((ATT_TILE, ATT_TILE)),
            pl.BlockSpec((1, width), lambda b, g, i: (0, g)),
        ],
        out_specs=pl.BlockSpec((1, ATT_TILE, width), lambda b, g, i: (b, i, g)),
        out_shape=jax.ShapeDtypeStruct((batch, seq, SB_WIDTH), BF16),
        scratch_shapes=[pltpu.VMEM((heads, ATT_TILE, LANES), BF16),
                        pltpu.VMEM((heads, ATT_TILE, ATT_TILE), F32),
                        pltpu.VMEM((heads, ATT_TILE, ATT_TILE), F32),
                        pltpu.VMEM((heads, ATT_TILE, ATT_TILE), F32),
                        pltpu.VMEM((heads, ATT_TILE, LANES), F32),
                        pltpu.VMEM((heads, ATT_TILE, LANES), F32)],
        compiler_params=pltpu.CompilerParams(
            dimension_semantics=("parallel", "parallel", "parallel"), vmem_limit_bytes=VMEM_LIMIT),
        name="stickbreak_attention",
    )(qkv, qkv, qkv, _reverse_cumsum_matrix(ATT_TILE), sb_norm_g)


def _hgrn_exponent_matrix():
    c = HG_CHUNK
    t = np.arange(c)[:, None]
    j = np.arange(c)[None, :]
    mats = [(j <= t)]
    for lvl in range(HG_LEVELS):
        m = c >> (lvl + 1)
        blk = t // m
        odd = (blk % 2) == 1
        prefix = (j >= blk * m) & (j <= t)
        suffix = (j > t) & (j <= blk * m + m - 1)
        mats.append(np.where(odd, prefix, suffix))
    z = np.concatenate(mats, axis=0).astype(np.float32)
    return jnp.asarray(np.concatenate([z, z], axis=1), BF16)


def _hgrn_kernel(q_ref, f_ref, i_ref, gate_ref, lbl_ref, ng_ref, zmat_ref, o_ref, state_ref):
    c = pl.program_id(1)
    ch = HG_CHUNK

    @pl.when(c == 0)
    def _():
        state_ref[...] = jnp.zeros_like(state_ref)

    logits = lbl_ref[...]
    mx = jnp.max(logits, axis=0, keepdims=True)
    ex = jnp.exp(logits - mx)
    lb = ex[0:1] / jnp.sum(ex, axis=0, keepdims=True)

    f_raw = f_ref[0]
    sig = 1.0 / (1.0 + jnp.exp(-f_raw))
    log_f = jnp.log(lb + (1.0 - lb) * sig)
    kk = (1.0 - lb) * (1.0 - sig)
    row = lax.broadcasted_iota(jnp.int32, (ch, 1), 0)
    valid = (c > 0) | (row >= ch - N_META)
    log_f = jnp.where(valid, log_f, 0.0)
    kk = jnp.where(valid, kk, 0.0)

    hi = log_f.astype(BF16)
    lo = (log_f - hi.astype(F32)).astype(BF16)
    expo = jnp.dot(zmat_ref[...], jnp.concatenate([hi, lo], axis=0),
                   preferred_element_type=F32)
    b = expo[0:ch]
    q = q_ref[0].astype(F32)
    v = i_ref[0]

    t_idx = lax.broadcasted_iota(jnp.int32, (ch, ch), 0)
    s_idx = lax.broadcasted_iota(jnp.int32, (ch, ch), 1)
    q_lvl, k_lvl, masks = [q.astype(BF16)], [kk.astype(BF16)], [t_idx == s_idx]
    for lvl in range(HG_LEVELS):
        m = ch >> (lvl + 1)
        decay = jnp.exp(expo[(lvl + 1) * ch:(lvl + 2) * ch])
        odd = ((row // m) % 2) == 1
        q_lvl.append(jnp.where(odd, q * decay, 0.0).astype(BF16))
        k_lvl.append(jnp.where(odd, 0.0, kk * decay).astype(BF16))
        masks.append((t_idx // (2 * m)) == (s_idx // (2 * m)))

    b_last = b[ch - 1:ch]
    q_inter = (q * jnp.exp(b)).astype(BF16)
    k_state = (kk * jnp.exp(b_last - b)).astype(BF16)
    state_decay = jnp.exp(b_last)

    outs = []
    for h in range(HG_HEADS):
        sl = slice(h * HG_DIM, (h + 1) * HG_DIM)
        scores = jnp.zeros((ch, ch), F32)
        for ql, kl, mask in zip(q_lvl, k_lvl, masks):
            part = lax.dot_general(ql[:, sl], kl[:, sl], NT_DIMS, preferred_element_type=F32)
            scores = scores + jnp.where(mask, part, 0.0)
        st = state_ref[h]
        o_h = (jnp.dot(scores.astype(BF16), v[:, sl], preferred_element_type=F32)
               + lax.dot_general(q_inter[:, sl], st.astype(BF16), NT_DIMS,
                                 preferred_element_type=F32))
        state_ref[h] = (st * state_decay[:, sl]
                        + lax.dot_general(v[:, sl], k_state[:, sl], TN_DIMS,
                                          preferred_element_type=F32))
        o_h = o_h * lax.rsqrt(jnp.mean(o_h * o_h, axis=-1, keepdims=True) + RMS_EPS)
        outs.append(o_h)
    o = jnp.concatenate(outs, axis=1) * ng_ref[...]
    gate = gate_ref[0]
    o_ref[0] = (o * (gate / (1.0 + jnp.exp(-gate)))).astype(o_ref.dtype)


def _hgrn(hq, hf, hi, hgate, lb_logits, hg_norm_g, seq):
    batch = hq.shape[0]
    first = FRONT // HG_CHUNK - 1
    n_chunks = seq // HG_CHUNK + 1
    in_spec = pl.BlockSpec((1, HG_CHUNK, HG_WIDTH), lambda b, c: (b, c + first, 0))
    zmat = _hgrn_exponent_matrix()
    return pl.pallas_call(
        _hgrn_kernel,
        grid=(batch, n_chunks),
        in_specs=[in_spec, in_spec, in_spec, in_spec,
                  _const_spec(lb_logits.shape), _const_spec((1, HG_WIDTH)),
                  _const_spec(zmat.shape)],
        out_specs=pl.BlockSpec((1, HG_CHUNK, HG_WIDTH), lambda b, c: (b, jnp.maximum(c - 1, 0), 0)),
        out_shape=jax.ShapeDtypeStruct((batch, seq, HG_WIDTH), BF16),
        scratch_shapes=[pltpu.VMEM((HG_HEADS, HG_DIM, HG_DIM), F32)],
        compiler_params=pltpu.CompilerParams(
            dimension_semantics=("parallel", "arbitrary"), vmem_limit_bytes=VMEM_LIMIT),
        name="hgrn2",
    )(hq, hf, hi, hgate, lb_logits, hg_norm_g, zmat)


def _ffn_kernel(x_ref, osb_ref, ohg_ref, wout_ref, g2_ref, wg_ref, wu_ref, wd_ref, gf_ref, o_ref):
    mix = (jnp.dot(osb_ref[0], wout_ref[0:SB_WIDTH, :], preferred_element_type=F32)
           + jnp.dot(ohg_ref[0], wout_ref[SB_WIDTH:, :], preferred_element_type=F32))
    h1 = x_ref[0] + mix
    u2 = (h1 * lax.rsqrt(jnp.mean(h1 * h1, axis=-1, keepdims=True) + RMS_EPS)
          * g2_ref[...]).astype(BF16)
    ffn = jnp.zeros_like(h1)
    width = D_FF // FFN_SPLIT
    for n in range(FFN_SPLIT):
        cols = slice(n * width, (n + 1) * width)
        gate = jnp.dot(u2, wg_ref[:, cols], preferred_element_type=F32)
        up = jnp.dot(u2, wu_ref[:, cols], preferred_element_type=F32)
        act = (gate / (1.0 + jnp.exp(-gate)) * up).astype(BF16)
        ffn = ffn + jnp.dot(act, wd_ref[cols, :], preferred_element_type=F32)
    h2 = h1 + ffn
    o_ref[0] = (h2 * lax.rsqrt(jnp.mean(h2 * h2, axis=-1, keepdims=True) + RMS_EPS)
                * gf_ref[...])


def _out_projection_ffn(x, o_sb, o_hg, w_out, norm2_g, w_gate, w_up, w_down, final_g):
    batch, seq, _ = x.shape
    row_spec = lambda width: pl.BlockSpec((1, FFN_ROWS, width), lambda b, j: (b, j, 0))
    resident = lambda shape: pl.BlockSpec(shape, lambda b, j: (0, 0), pipeline_mode=pl.Buffered(1))
    return pl.pallas_call(
        _ffn_kernel,
        grid=(batch, seq // FFN_ROWS),
        in_specs=[row_spec(D_MODEL), row_spec(SB_WIDTH), row_spec(HG_WIDTH),
                  resident((D_MODEL, D_MODEL)), resident((1, D_MODEL)),
                  resident((D_MODEL, D_FF)), resident((D_MODEL, D_FF)), resident((D_FF, D_MODEL)),
                  resident((1, D_MODEL))],
        out_specs=row_spec(D_MODEL),
        out_shape=jax.ShapeDtypeStruct((batch, seq, D_MODEL), x.dtype),
        compiler_params=pltpu.CompilerParams(
            dimension_semantics=("parallel", "parallel"), vmem_limit_bytes=VMEM_LIMIT),
        name="out_projection_ffn",
    )(x, o_sb, o_hg, w_out, norm2_g, w_gate, w_up, w_down, final_g)


def kernel(x, meta_tokens, norm1_g, w_in, sb_norm_g, hg_norm_g, hg_lb_logits, w_out, norm2_g,
           w_gate, w_up, w_down, final_norm_g):
    batch, seq, d_model = x.shape
    assert d_model == D_MODEL and seq % FFN_ROWS == 0 and seq % ATT_TILE == 0
    assert norm1_g.shape[0] == 1, "single-layer block"
    head = jnp.concatenate(
        [jnp.zeros((FIRST_VALID, D_MODEL), x.dtype), meta_tokens.astype(x.dtype)], axis=0)
    row = lambda g: g.reshape(1, -1).astype(F32)

    qkv, hq, hf, hi, hgate = _input_projection(x, head, row(norm1_g[0]), w_in[0].astype(BF16))
    o_sb = _attention(qkv, row(sb_norm_g[0]), seq)
    o_hg = _hgrn(hq, hf, hi, hgate, hg_lb_logits.astype(F32), row(hg_norm_g[0]), seq)
    return _out_projection_ffn(
        x, o_sb, o_hg, w_out[0].astype(BF16), row(norm2_g[0]), w_gate[0].astype(BF16),
        w_up[0].astype(BF16), w_down[0].astype(BF16), row(final_norm_g))
```

```python
import functools

import jax
import jax.numpy as jnp
import numpy as np
from jax import lax
from jax.experimental import pallas as pl
from jax.experimental.pallas import tpu as pltpu

F32 = jnp.float32
BF16 = jnp.bfloat16

D_MODEL = 1024
N_META = 16
SB_HEADS = 8
SB_HEAD_DIM = 64
SB_WIDTH = SB_HEADS * SB_HEAD_DIM
HG_HEADS = 4
HG_DIM = 128
HG_WIDTH = HG_HEADS * HG_DIM
IN_WIDTH = 3 * SB_WIDTH + 4 * HG_WIDTH
D_FF = 2816
RMS_EPS = 1e-6

LANES = 128
SUBLANES = 8
FRONT = 256
FIRST_VALID = FRONT - N_META
ATT_TILE = 256
ATT_PAIRS = 4
HG_CHUNK = 128
HG_LEVELS = 7
HG_STEP_CHUNKS = 2
PROJ_ROWS = 256
FFN_ROWS = 512
FFN_SPLIT = 2
VMEM_LIMIT = 56 * 1024 * 1024
Q_SCALE = float(np.log2(np.e)) * SB_HEAD_DIM ** -0.5
MASKED = -1e9
SP_LINEAR = 64.0

NT_DIMS = (((1,), (1,)), ((), ()))
TN_DIMS = (((0,), (0,)), ((), ()))


def _const_spec(shape):
    return pl.BlockSpec(shape, lambda *_: (0,) * len(shape))


def _inproj_kernel(x_ref, head_ref, g_ref, w_ref, qkv_ref, hq_ref, hf_ref, hi_ref, hg_ref):
    j = pl.program_id(1)
    h = jnp.where(j == 0, head_ref[...], x_ref[0])
    u = h * lax.rsqrt(jnp.mean(h * h, axis=-1, keepdims=True) + RMS_EPS) * g_ref[...]
    ub = u.astype(BF16)

    def proj(lo, width):
        return jnp.dot(ub, w_ref[:, lo:lo + width], preferred_element_type=F32)

    qkv_ref[0, :, 0:SB_WIDTH] = (proj(0, SB_WIDTH) * Q_SCALE).astype(BF16)
    qkv_ref[0, :, SB_WIDTH:3 * SB_WIDTH] = proj(SB_WIDTH, 2 * SB_WIDTH).astype(BF16)
    base = 3 * SB_WIDTH
    hq_ref[0] = proj(base, HG_WIDTH).astype(BF16)
    hf_ref[0] = proj(base + HG_WIDTH, HG_WIDTH)
    hi_ref[0] = proj(base + 2 * HG_WIDTH, HG_WIDTH).astype(BF16)
    hg_ref[0] = proj(base + 3 * HG_WIDTH, HG_WIDTH)


def _input_projection(x, head, norm_g, w_in_bf16):
    batch, seq, _ = x.shape
    n_tiles = (FRONT + seq) // PROJ_ROWS
    rows = FRONT + seq
    row_spec = lambda width: pl.BlockSpec((1, PROJ_ROWS, width), lambda b, j: (b, j, 0))
    return pl.pallas_call(
        _inproj_kernel,
        grid=(batch, n_tiles),
        in_specs=[
            pl.BlockSpec((1, PROJ_ROWS, D_MODEL), lambda b, j: (b, jnp.maximum(j - 1, 0), 0)),
            _const_spec((PROJ_ROWS, D_MODEL)),
            _const_spec((1, D_MODEL)),
            _const_spec((D_MODEL, IN_WIDTH)),
        ],
        out_specs=[row_spec(3 * SB_WIDTH), row_spec(HG_WIDTH), row_spec(HG_WIDTH),
                   row_spec(HG_WIDTH), row_spec(HG_WIDTH)],
        out_shape=[
            jax.ShapeDtypeStruct((batch, rows, 3 * SB_WIDTH), BF16),
            jax.ShapeDtypeStruct((batch, rows, HG_WIDTH), BF16),
            jax.ShapeDtypeStruct((batch, rows, HG_WIDTH), F32),
            jax.ShapeDtypeStruct((batch, rows, HG_WIDTH), BF16),
            jax.ShapeDtypeStruct((batch, rows, HG_WIDTH), F32),
        ],
        compiler_params=pltpu.CompilerParams(
            dimension_semantics=("parallel", "parallel"), vmem_limit_bytes=VMEM_LIMIT),
        name="in_projection",
    )(x, head, norm_g, w_in_bf16)


def _attn_kernel(q_ref, k_ref, v_ref, tri_ref, g_ref, o_ref,
                 qh_ref, z_ref, y_ref, c_ref, sum_ref, acc_ref):
    tile = ATT_TILE
    n_chunks = k_ref.shape[1] // tile
    n_heads = z_ref.shape[0]
    i = pl.program_id(2) + FRONT // ATT_TILE
    last = i - FIRST_VALID // ATT_TILE
    lane = lax.broadcasted_iota(jnp.int32, (1, LANES), 1)
    first_head = lane < SB_HEAD_DIM

    def pair_lanes(h):
        return slice((h // 2) * LANES, (h // 2 + 1) * LANES)

    def rows_of(c):
        return pl.ds(pl.multiple_of(c * tile, tile), tile)

    def minus_row(a, row8):
        return (a.reshape(tile // SUBLANES, SUBLANES, tile) - row8[None]).reshape(tile, tile)

    def step(n, masked, stages="ABC"):
        c_new = i - n
        c_a = jnp.maximum(c_new, 0)
        c_c = jnp.clip(c_new + 2, 0, n_chunks - 1)
        if masked:
            s_pos = c_new * tile + lax.broadcasted_iota(jnp.int32, (tile, 1), 0)
            t_pos = i * tile + lax.broadcasted_iota(jnp.int32, (1, tile), 1)
            visible = (s_pos < t_pos) & (s_pos >= FIRST_VALID)
        for h in range(n_heads):
            if "C" in stages:
                local = c_ref[h]
                w = jnp.exp2((y_ref[h] - local).astype(BF16))
                acc_ref[h] += lax.dot_general(v_ref[0, rows_of(c_c), pair_lanes(h)], w, TN_DIMS,
                                              preferred_element_type=F32)
                later = sum_ref[h] + jnp.broadcast_to(local[0:1, :], (SUBLANES, tile))
                sum_ref[h] = later
            if "B" in stages:
                z = z_ref[h]
                if "C" not in stages:
                    later = sum_ref[h]
            if "A" in stages:
                z_new = lax.dot_general(k_ref[0, rows_of(c_a), pair_lanes(h)], qh_ref[h], NT_DIMS,
                                        preferred_element_type=F32)
                if masked:
                    z_new = jnp.where(visible, z_new, MASKED)
                z_ref[h] = z_new
            if "B" in stages:
                zb = z.astype(BF16)
                grown = 1.0 + jnp.exp2(jnp.minimum(zb, SP_LINEAR))
                sp = jnp.maximum(zb, jnp.log2(grown.astype(F32)).astype(BF16))
                y_ref[h] = minus_row(z, later)
                c_ref[h] = jnp.dot(tri_ref[...], sp, preferred_element_type=F32)

    for h in range(n_heads):
        q_pair = q_ref[0, :, pair_lanes(h)]
        qh_ref[h] = jnp.where(first_head == (h % 2 == 0), q_pair, jnp.zeros_like(q_pair))
    sum_ref[...] = jnp.zeros(sum_ref.shape, F32)
    acc_ref[...] = jnp.zeros(acc_ref.shape, F32)
    step(0, True, "A")
    step(1, True, "AB")

    @pl.when(last > 1)
    def _():
        @pl.loop(2, last)
        def _(n):
            step(n, False)

        step(last, True)

    step(last + 1, False, "BC")
    step(last + 2, False, "C")

    for p in range(n_heads // 2):
        o = jnp.concatenate([acc_ref[2 * p, 0:SB_HEAD_DIM], acc_ref[2 * p + 1, SB_HEAD_DIM:LANES]],
                            axis=0).T
        sq = o * o
        ss_first = jnp.sum(jnp.where(first_head, sq, 0.0), axis=-1, keepdims=True)
        ss_second = jnp.sum(sq, axis=-1, keepdims=True) - ss_first
        mean_sq = jnp.where(first_head, ss_first, ss_second) * (1.0 / SB_HEAD_DIM)
        lanes = slice(p * LANES, (p + 1) * LANES)
        o_ref[0, :, lanes] = (o * lax.rsqrt(mean_sq + RMS_EPS) * g_ref[:, lanes]).astype(o_ref.dtype)


def _reverse_cumsum_matrix(n):
    j = np.arange(n)[:, None]
    s = np.arange(n)[None, :]
    return jnp.asarray((s >= j).astype(np.float32), BF16)


def _attention(qkv, sb_norm_g, seq):
    batch, rows, _ = qkv.shape
    width = ATT_PAIRS * LANES
    groups = SB_WIDTH // width
    n_q = seq // ATT_TILE
    heads = 2 * ATT_PAIRS
    return pl.pallas_call(
        _attn_kernel,
        grid=(batch, groups, n_q),
        in_specs=[
            pl.BlockSpec((1, ATT_TILE, width), lambda b, g, i: (b, i + FRONT // ATT_TILE, g)),
            pl.BlockSpec((1, rows, width), lambda b, g, i: (b, 0, groups + g),
                         pipeline_mode=pl.Buffered(1)),
            pl.BlockSpec((1, rows, width), lambda b, g, i: (b, 0, 2 * groups + g),
                         pipeline_mode=pl.Buffered(1)),
            _const_spec((ATT_TILE, ATT_TILE)),
            pl.BlockSpec((1, width), lambda b, g, i: (0, g)),
        ],
        out_specs=pl.BlockSpec((1, ATT_TILE, width), lambda b, g, i: (b, i, g)),
        out_shape=jax.ShapeDtypeStruct((batch, seq, SB_WIDTH), BF16),
        scratch_shapes=[pltpu.VMEM((heads, ATT_TILE, LANES), BF16),
                        pltpu.VMEM((heads, ATT_TILE, ATT_TILE), F32),
                        pltpu.VMEM((heads, ATT_TILE, ATT_TILE), F32),
                        pltpu.VMEM((heads, ATT_TILE, ATT_TILE), F32),
                        pltpu.VMEM((heads, SUBLANES, ATT_TILE), F32),
                        pltpu.VMEM((heads, LANES, ATT_TILE), F32)],
        compiler_params=pltpu.CompilerParams(
            dimension_semantics=("parallel", "parallel", "parallel"), vmem_limit_bytes=VMEM_LIMIT),
        name="stickbreak_attention",
    )(qkv, qkv, qkv, _reverse_cumsum_matrix(ATT_TILE), sb_norm_g)


def _hgrn_exponent_matrix():
    c = HG_CHUNK
    t = np.arange(c)[:, None]
    j = np.arange(c)[None, :]
    mats = [(j <= t)]
    for lvl in range(HG_LEVELS):
        m = c >> (lvl + 1)
        blk = t // m
        odd = (blk % 2) == 1
        prefix = (j >= blk * m) & (j <= t)
        suffix = (j > t) & (j <= blk * m + m - 1)
        mats.append(np.where(odd, prefix, suffix))
    z = np.concatenate(mats, axis=0).astype(np.float32)
    return jnp.asarray(np.concatenate([z, z], axis=1), BF16)


def _hgrn_level_masks():
    c = HG_CHUNK
    t = np.arange(c)[:, None]
    s = np.arange(c)[None, :]
    masks = [t == s]
    for lvl in range(HG_LEVELS):
        m = c >> (lvl + 1)
        masks.append(((t // m) == (s // m) + 1) & (((s // m) % 2) == 0))
    return jnp.asarray(np.stack(masks).astype(np.float32))


def _hgrn_kernel(q_ref, f_ref, i_ref, gate_ref, lbl_ref, ng_ref, zmat_ref, mask_ref, o_ref,
                 state_ref):
    j = pl.program_id(1)
    ch = HG_CHUNK
    rows = HG_STEP_CHUNKS * ch

    @pl.when(j == 0)
    def _():
        state_ref[...] = jnp.zeros_like(state_ref)

    logits = lbl_ref[...]
    mx = jnp.max(logits, axis=0, keepdims=True)
    ex = jnp.exp(logits - mx)
    lb = ex[0:1] / jnp.sum(ex, axis=0, keepdims=True)

    f_raw = f_ref[0]
    sig = 0.5 + 0.5 * jnp.tanh(0.5 * f_raw)
    log_f = jnp.log2(lb + (1.0 - lb) * sig)
    kk = (1.0 - lb) * (1.0 - sig)
    row_all = lax.broadcasted_iota(jnp.int32, (rows, 1), 0)
    valid = (j > 0) | (row_all >= FIRST_VALID)
    log_f = jnp.where(valid, log_f, 0.0)
    kk = jnp.where(valid, kk, 0.0)
    hi = log_f.astype(BF16)
    lo = (log_f - hi.astype(F32)).astype(BF16)
    q_all = q_ref[0].astype(F32)
    v_all = i_ref[0]

    chunks = [slice(n * ch, (n + 1) * ch) for n in range(HG_STEP_CHUNKS)]
    expo = [jnp.dot(zmat_ref[...], jnp.concatenate([hi[r], lo[r]], axis=0),
                    preferred_element_type=F32) for r in chunks]

    row = lax.broadcasted_iota(jnp.int32, (ch, 1), 0)

    sides, q_bf, k_bf, q_inter, k_state, state_decay = [], [], [], [], [], []
    for n, r in enumerate(chunks):
        q, k = q_all[r], kk[r]
        level_sides = []
        for lvl in range(HG_LEVELS):
            m = ch >> (lvl + 1)
            decay = jnp.exp2(expo[n][(lvl + 1) * ch:(lvl + 2) * ch])
            if m % SUBLANES == 0:
                both = jnp.concatenate([(q if blk % 2 else k)[blk * m:(blk + 1) * m]
                                        for blk in range(ch // m)], axis=0)
            else:
                both = jnp.where(((row // m) % 2) == 1, q, k)
            level_sides.append((both * decay).astype(BF16))
        sides.append(level_sides)
        q_bf.append(q.astype(BF16))
        k_bf.append(k.astype(BF16))
        b = expo[n][0:ch]
        b_last = b[ch - 1:ch]
        q_inter.append((q * jnp.exp2(b)).astype(BF16))
        k_state.append((k * jnp.exp2(b_last - b)).astype(BF16))
        state_decay.append(jnp.exp2(b_last))

    heads = [slice(h * HG_DIM, (h + 1) * HG_DIM) for h in range(HG_HEADS)]
    scores = []
    for n in range(HG_STEP_CHUNKS):
        parts = [[lax.dot_general(x[:, sl], y[:, sl], NT_DIMS, preferred_element_type=F32)
                  for x, y in [(q_bf[n], k_bf[n])] + [(sd, sd) for sd in sides[n]]]
                 for sl in heads]
        per_head = []
        for h in range(HG_HEADS):
            sc = parts[h][0] * mask_ref[0]
            for lvl in range(HG_LEVELS):
                sc = sc + parts[h][lvl + 1] * mask_ref[lvl + 1]
            per_head.append(sc.astype(BF16))
        scores.append(per_head)

    intra = [[jnp.dot(scores[n][h], v_all[r, sl], preferred_element_type=F32)
              for h, sl in enumerate(heads)] for n, r in enumerate(chunks)]
    kv = [[lax.dot_general(v_all[r, sl], k_state[n][:, sl], TN_DIMS, preferred_element_type=F32)
           for sl in heads] for n, r in enumerate(chunks)]

    for n, r in enumerate(chunks):
        outs = []
        for h, sl in enumerate(heads):
            st = state_ref[h]
            o_h = intra[n][h] + lax.dot_general(q_inter[n][:, sl], st.astype(BF16), NT_DIMS,
                                                preferred_element_type=F32)
            state_ref[h] = st * state_decay[n][:, sl] + kv[n][h]
            outs.append(o_h * lax.rsqrt(jnp.mean(o_h * o_h, axis=-1, keepdims=True) + RMS_EPS))
        o = jnp.concatenate(outs, axis=1) * ng_ref[...]
        half_gate = 0.5 * gate_ref[0, r, :]
        o_ref[0, r, :] = (o * (half_gate + half_gate * jnp.tanh(half_gate))).astype(o_ref.dtype)


def _hgrn(hq, hf, hi, hgate, lb_logits, hg_norm_g, seq):
    batch = hq.shape[0]
    rows = HG_STEP_CHUNKS * HG_CHUNK
    assert FRONT == rows, "the first grid step must hold exactly the rows in front of x"
    in_spec = pl.BlockSpec((1, rows, HG_WIDTH), lambda b, j: (b, j, 0))
    zmat = _hgrn_exponent_matrix()
    masks = _hgrn_level_masks()
    return pl.pallas_call(
        _hgrn_kernel,
        grid=(batch, (FRONT + seq) // rows),
        in_specs=[in_spec, in_spec, in_spec, in_spec,
                  _const_spec(lb_logits.shape), _const_spec((1, HG_WIDTH)),
                  _const_spec(zmat.shape), _const_spec(masks.shape)],
        out_specs=pl.BlockSpec((1, rows, HG_WIDTH), lambda b, j: (b, jnp.maximum(j - 1, 0), 0)),
        out_shape=jax.ShapeDtypeStruct((batch, seq, HG_WIDTH), BF16),
        scratch_shapes=[pltpu.VMEM((HG_HEADS, HG_DIM, HG_DIM), F32)],
        compiler_params=pltpu.CompilerParams(
            dimension_semantics=("parallel", "arbitrary"), vmem_limit_bytes=VMEM_LIMIT),
        name="hgrn2",
    )(hq, hf, hi, hgate, lb_logits, hg_norm_g, zmat, masks)


def _ffn_kernel(x_ref, osb_ref, ohg_ref, wout_ref, g2_ref, wg_ref, wu_ref, wd_ref, gf_ref, o_ref):
    mix = (jnp.dot(osb_ref[0], wout_ref[0:SB_WIDTH, :], preferred_element_type=F32)
           + jnp.dot(ohg_ref[0], wout_ref[SB_WIDTH:, :], preferred_element_type=F32))
    h1 = x_ref[0] + mix
    u2 = (h1 * lax.rsqrt(jnp.mean(h1 * h1, axis=-1, keepdims=True) + RMS_EPS)
          * g2_ref[...]).astype(BF16)
    ffn = jnp.zeros_like(h1)
    width = D_FF // FFN_SPLIT
    for n in range(FFN_SPLIT):
        cols = slice(n * width, (n + 1) * width)
        gate = jnp.dot(u2, wg_ref[:, cols], preferred_element_type=F32)
        up = jnp.dot(u2, wu_ref[:, cols], preferred_element_type=F32)
        half_gate = 0.5 * gate
        act = ((half_gate + half_gate * jnp.tanh(half_gate)) * up).astype(BF16)
        ffn = ffn + jnp.dot(act, wd_ref[cols, :], preferred_element_type=F32)
    h2 = h1 + ffn
    o_ref[0] = (h2 * lax.rsqrt(jnp.mean(h2 * h2, axis=-1, keepdims=True) + RMS_EPS)
                * gf_ref[...])


def _out_projection_ffn(x, o_sb, o_hg, w_out, norm2_g, w_gate, w_up, w_down, final_g):
    batch, seq, _ = x.shape
    row_spec = lambda width: pl.BlockSpec((1, FFN_ROWS, width), lambda b, j: (b, j, 0))
    resident = lambda shape: pl.BlockSpec(shape, lambda b, j: (0, 0), pipeline_mode=pl.Buffered(1))
    return pl.pallas_call(
        _ffn_kernel,
        grid=(batch, seq // FFN_ROWS),
        in_specs=[row_spec(D_MODEL), row_spec(SB_WIDTH), row_spec(HG_WIDTH),
                  resident((D_MODEL, D_MODEL)), resident((1, D_MODEL)),
                  resident((D_MODEL, D_FF)), resident((D_MODEL, D_FF)), resident((D_FF, D_MODEL)),
                  resident((1, D_MODEL))],
        out_specs=row_spec(D_MODEL),
        out_shape=jax.ShapeDtypeStruct((batch, seq, D_MODEL), x.dtype),
        compiler_params=pltpu.CompilerParams(
            dimension_semantics=("parallel", "parallel"), vmem_limit_bytes=VMEM_LIMIT),
        name="out_projection_ffn",
    )(x, o_sb, o_hg, w_out, norm2_g, w_gate, w_up, w_down, final_g)


def kernel(x, meta_tokens, norm1_g, w_in, sb_norm_g, hg_norm_g, hg_lb_logits, w_out, norm2_g,
           w_gate, w_up, w_down, final_norm_g):
    batch, seq, d_model = x.shape
    assert d_model == D_MODEL and seq % FFN_ROWS == 0 and seq % ATT_TILE == 0
    assert norm1_g.shape[0] == 1, "single-layer block"
    head = jnp.concatenate(
        [jnp.zeros((FIRST_VALID, D_MODEL), x.dtype), meta_tokens.astype(x.dtype)], axis=0)
    row = lambda g: g.reshape(1, -1).astype(F32)

    qkv, hq, hf, hi, hgate = _input_projection(x, head, row(norm1_g[0]), w_in[0].astype(BF16))
    o_sb = _attention(qkv, row(sb_norm_g[0]), seq)
    o_hg = _hgrn(hq, hf, hi, hgate, hg_lb_logits.astype(F32), row(hg_norm_g[0]), seq)
    return _out_projection_ffn(
        x, o_sb, o_hg, w_out[0].astype(BF16), row(norm2_g[0]), w_gate[0].astype(BF16),
        w_up[0].astype(BF16), w_down[0].astype(BF16), row(final_norm_g))
```

```python
import functools

import jax
import jax.numpy as jnp
import numpy as np
from jax import lax
from jax.experimental import pallas as pl
from jax.experimental.pallas import tpu as pltpu

F32 = jnp.float32
BF16 = jnp.bfloat16

D_MODEL = 1024
N_META = 16
SB_HEADS = 8
SB_HEAD_DIM = 64
SB_WIDTH = SB_HEADS * SB_HEAD_DIM
HG_HEADS = 4
HG_DIM = 128
HG_WIDTH = HG_HEADS * HG_DIM
IN_WIDTH = 3 * SB_WIDTH + 4 * HG_WIDTH
D_FF = 2816
RMS_EPS = 1e-6

LANES = 128
SUBLANES = 8
FRONT = 256
FIRST_VALID = FRONT - N_META
ATT_TILE = 256
ATT_PAIRS = 4
HG_CHUNK = 128
HG_LEVELS = 7
HG_STEP_CHUNKS = 2
PROJ_ROWS = 256
FFN_ROWS = 512
FFN_SPLIT = 2
VMEM_LIMIT = 56 * 1024 * 1024
Q_SCALE = float(np.log2(np.e)) * SB_HEAD_DIM ** -0.5
MASKED = -1e9
SP_LINEAR = 64.0
EXIT_SUM = 160.0

NT_DIMS = (((1,), (1,)), ((), ()))
TN_DIMS = (((0,), (0,)), ((), ()))


def _const_spec(shape):
    return pl.BlockSpec(shape, lambda *_: (0,) * len(shape))


def _inproj_kernel(x_ref, head_ref, g_ref, w_ref, qkv_ref, hq_ref, hf_ref, hi_ref, hg_ref):
    j = pl.program_id(1)
    h = jnp.where(j == 0, head_ref[...], x_ref[0])
    u = h * lax.rsqrt(jnp.mean(h * h, axis=-1, keepdims=True) + RMS_EPS) * g_ref[...]
    ub = u.astype(BF16)

    def proj(lo, width):
        return jnp.dot(ub, w_ref[:, lo:lo + width], preferred_element_type=F32)

    qkv_ref[0, :, 0:SB_WIDTH] = (proj(0, SB_WIDTH) * Q_SCALE).astype(BF16)
    qkv_ref[0, :, SB_WIDTH:3 * SB_WIDTH] = proj(SB_WIDTH, 2 * SB_WIDTH).astype(BF16)
    base = 3 * SB_WIDTH
    hq_ref[0] = proj(base, HG_WIDTH).astype(BF16)
    hf_ref[0] = proj(base + HG_WIDTH, HG_WIDTH)
    hi_ref[0] = proj(base + 2 * HG_WIDTH, HG_WIDTH).astype(BF16)
    hg_ref[0] = proj(base + 3 * HG_WIDTH, HG_WIDTH)


def _input_projection(x, head, norm_g, w_in_bf16):
    batch, seq, _ = x.shape
    n_tiles = (FRONT + seq) // PROJ_ROWS
    rows = FRONT + seq
    row_spec = lambda width: pl.BlockSpec((1, PROJ_ROWS, width), lambda b, j: (b, j, 0))
    return pl.pallas_call(
        _inproj_kernel,
        grid=(batch, n_tiles),
        in_specs=[
            pl.BlockSpec((1, PROJ_ROWS, D_MODEL), lambda b, j: (b, jnp.maximum(j - 1, 0), 0)),
            _const_spec((PROJ_ROWS, D_MODEL)),
            _const_spec((1, D_MODEL)),
            _const_spec((D_MODEL, IN_WIDTH)),
        ],
        out_specs=[row_spec(3 * SB_WIDTH), row_spec(HG_WIDTH), row_spec(HG_WIDTH),
                   row_spec(HG_WIDTH), row_spec(HG_WIDTH)],
        out_shape=[
            jax.ShapeDtypeStruct((batch, rows, 3 * SB_WIDTH), BF16),
            jax.ShapeDtypeStruct((batch, rows, HG_WIDTH), BF16),
            jax.ShapeDtypeStruct((batch, rows, HG_WIDTH), F32),
            jax.ShapeDtypeStruct((batch, rows, HG_WIDTH), BF16),
            jax.ShapeDtypeStruct((batch, rows, HG_WIDTH), F32),
        ],
        compiler_params=pltpu.CompilerParams(
            dimension_semantics=("parallel", "parallel"), vmem_limit_bytes=VMEM_LIMIT),
        name="in_projection",
    )(x, head, norm_g, w_in_bf16)


def _attn_kernel(q_ref, k_ref, v_ref, tri_ref, g_ref, o_ref,
                 qh_ref, z_ref, y_ref, c_ref, sum_ref, acc_ref):
    tile = ATT_TILE
    n_chunks = k_ref.shape[1] // tile
    n_heads = z_ref.shape[0]
    i = pl.program_id(2) + FRONT // ATT_TILE
    last = i - FIRST_VALID // ATT_TILE
    lane = lax.broadcasted_iota(jnp.int32, (1, LANES), 1)
    first_head = lane < SB_HEAD_DIM

    def pair_lanes(h):
        return slice((h // 2) * LANES, (h // 2 + 1) * LANES)

    def rows_of(c):
        return pl.ds(pl.multiple_of(c * tile, tile), tile)

    def minus_row(a, row8):
        return (a.reshape(tile // SUBLANES, SUBLANES, tile) - row8[None]).reshape(tile, tile)

    def step(n, masked, stages="ABC"):
        c_new = i - n
        c_a = jnp.maximum(c_new, 0)
        c_c = jnp.clip(c_new + 2, 0, n_chunks - 1)
        if masked:
            s_pos = c_new * tile + lax.broadcasted_iota(jnp.int32, (tile, 1), 0)
            t_pos = i * tile + lax.broadcasted_iota(jnp.int32, (1, tile), 1)
            visible = (s_pos < t_pos) & (s_pos >= FIRST_VALID)
        for h in range(n_heads):
            if "C" in stages:
                local = c_ref[h]
                w = jnp.exp2((y_ref[h] - local).astype(BF16))
                acc_ref[h] += lax.dot_general(v_ref[0, rows_of(c_c), pair_lanes(h)], w, TN_DIMS,
                                              preferred_element_type=F32)
                later = sum_ref[h] + jnp.broadcast_to(local[0:1, :], (SUBLANES, tile))
                sum_ref[h] = later
            if "B" in stages:
                z = z_ref[h]
                if "C" not in stages:
                    later = sum_ref[h]
            if "A" in stages:
                z_new = lax.dot_general(k_ref[0, rows_of(c_a), pair_lanes(h)], qh_ref[h], NT_DIMS,
                                        preferred_element_type=F32)
                if masked:
                    z_new = jnp.where(visible, z_new, MASKED)
                z_ref[h] = z_new
            if "B" in stages:
                zb = z.astype(BF16)
                grown = 1.0 + jnp.exp2(jnp.minimum(zb, SP_LINEAR))
                sp = jnp.maximum(zb, jnp.log2(grown.astype(F32)).astype(BF16))
                y_ref[h] = minus_row(z, later)
                c_ref[h] = jnp.dot(tri_ref[...], sp, preferred_element_type=F32)

    for h in range(n_heads):
        q_pair = q_ref[0, :, pair_lanes(h)]
        qh_ref[h] = jnp.where(first_head == (h % 2 == 0), q_pair, jnp.zeros_like(q_pair))
    sum_ref[...] = jnp.zeros(sum_ref.shape, F32)
    acc_ref[...] = jnp.zeros(acc_ref.shape, F32)
    step(0, True, "A")
    step(1, True, "AB")

    def sweep(carry):
        n, _ = carry
        step(n, False)
        return n + 1, jnp.min(sum_ref[...])

    _, smallest_sum = lax.while_loop(lambda carry: (carry[0] < last) & (carry[1] < EXIT_SUM),
                                     sweep, (jnp.int32(2), jnp.float32(0.0)))

    @pl.when(smallest_sum < EXIT_SUM)
    def _():
        @pl.when(last > 1)
        def _():
            step(last, True)

        step(last + 1, False, "BC")
        step(last + 2, False, "C")

    for p in range(n_heads // 2):
        o = jnp.concatenate([acc_ref[2 * p, 0:SB_HEAD_DIM], acc_ref[2 * p + 1, SB_HEAD_DIM:LANES]],
                            axis=0).T
        sq = o * o
        ss_first = jnp.sum(jnp.where(first_head, sq, 0.0), axis=-1, keepdims=True)
        ss_second = jnp.sum(sq, axis=-1, keepdims=True) - ss_first
        mean_sq = jnp.where(first_head, ss_first, ss_second) * (1.0 / SB_HEAD_DIM)
        lanes = slice(p * LANES, (p + 1) * LANES)
        o_ref[0, :, lanes] = (o * lax.rsqrt(mean_sq + RMS_EPS) * g_ref[:, lanes]).astype(o_ref.dtype)


def _reverse_cumsum_matrix(n):
    j = np.arange(n)[:, None]
    s = np.arange(n)[None, :]
    return jnp.asarray((s >= j).astype(np.float32), BF16)


def _attention(qkv, sb_norm_g, seq):
    batch, rows, _ = qkv.shape
    width = ATT_PAIRS * LANES
    groups = SB_WIDTH // width
    n_q = seq // ATT_TILE
    heads = 2 * ATT_PAIRS
    return pl.pallas_call(
        _attn_kernel,
        grid=(batch, groups, n_q),
        in_specs=[
            pl.BlockSpec((1, ATT_TILE, width), lambda b, g, i: (b, i + FRONT // ATT_TILE, g)),
            pl.BlockSpec((1, rows, width), lambda b, g, i: (b, 0, groups + g),
                         pipeline_mode=pl.Buffered(1)),
            pl.BlockSpec((1, rows, width), lambda b, g, i: (b, 0, 2 * groups + g),
                         pipeline_mode=pl.Buffered(1)),
            _const_spec((ATT_TILE, ATT_TILE)),
            pl.BlockSpec((1, width), lambda b, g, i: (0, g)),
        ],
        out_specs=pl.BlockSpec((1, ATT_TILE, width), lambda b, g, i: (b, i, g)),
        out_shape=jax.ShapeDtypeStruct((batch, seq, SB_WIDTH), BF16),
        scratch_shapes=[pltpu.VMEM((heads, ATT_TILE, LANES), BF16),
                        pltpu.VMEM((heads, ATT_TILE, ATT_TILE), F32),
                        pltpu.VMEM((heads, ATT_TILE, ATT_TILE), F32),
                        pltpu.VMEM((heads, ATT_TILE, ATT_TILE), F32),
                        pltpu.VMEM((heads, SUBLANES, ATT_TILE), F32),
                        pltpu.VMEM((heads, LANES, ATT_TILE), F32)],
        compiler_params=pltpu.CompilerParams(
            dimension_semantics=("parallel", "parallel", "parallel"), vmem_limit_bytes=VMEM_LIMIT),
        name="stickbreak_attention",
    )(qkv, qkv, qkv, _reverse_cumsum_matrix(ATT_TILE), sb_norm_g)


def _hgrn_exponent_matrix():
    c = HG_CHUNK
    t = np.arange(c)[:, None]
    j = np.arange(c)[None, :]
    mats = [(j <= t)]
    for lvl in range(HG_LEVELS):
        m = c >> (lvl + 1)
        blk = t // m
        odd = (blk % 2) == 1
        prefix = (j >= blk * m) & (j <= t)
        suffix = (j > t) & (j <= blk * m + m - 1)
        mats.append(np.where(odd, prefix, suffix))
    z = np.concatenate(mats, axis=0).astype(np.float32)
    return jnp.asarray(np.concatenate([z, z], axis=1), BF16)


def _hgrn_level_masks():
    c = HG_CHUNK
    t = np.arange(c)[:, None]
    s = np.arange(c)[None, :]
    masks = [t == s]
    for lvl in range(HG_LEVELS):
        m = c >> (lvl + 1)
        masks.append(((t // m) == (s // m) + 1) & (((s // m) % 2) == 0))
    return jnp.asarray(np.stack(masks).astype(np.float32))


def _hgrn_kernel(q_ref, f_ref, i_ref, gate_ref, lbl_ref, ng_ref, zmat_ref, mask_ref, o_ref,
                 state_ref):
    j = pl.program_id(1)
    ch = HG_CHUNK
    rows = HG_STEP_CHUNKS * ch

    @pl.when(j == 0)
    def _():
        state_ref[...] = jnp.zeros_like(state_ref)

    logits = lbl_ref[...]
    mx = jnp.max(logits, axis=0, keepdims=True)
    ex = jnp.exp(logits - mx)
    lb = ex[0:1] / jnp.sum(ex, axis=0, keepdims=True)

    f_raw = f_ref[0]
    sig = 0.5 + 0.5 * jnp.tanh(0.5 * f_raw)
    log_f = jnp.log2(lb + (1.0 - lb) * sig)
    kk = (1.0 - lb) * (1.0 - sig)
    row_all = lax.broadcasted_iota(jnp.int32, (rows, 1), 0)
    valid = (j > 0) | (row_all >= FIRST_VALID)
    log_f = jnp.where(valid, log_f, 0.0)
    kk = jnp.where(valid, kk, 0.0)
    hi = log_f.astype(BF16)
    lo = (log_f - hi.astype(F32)).astype(BF16)
    q_all = q_ref[0].astype(F32)
    v_all = i_ref[0]

    chunks = [slice(n * ch, (n + 1) * ch) for n in range(HG_STEP_CHUNKS)]
    expo = [jnp.dot(zmat_ref[...], jnp.concatenate([hi[r], lo[r]], axis=0),
                    preferred_element_type=F32) for r in chunks]

    row = lax.broadcasted_iota(jnp.int32, (ch, 1), 0)

    sides, q_bf, k_bf, q_inter, k_state, state_decay = [], [], [], [], [], []
    for n, r in enumerate(chunks):
        q, k = q_all[r], kk[r]
        level_sides = []
        for lvl in range(HG_LEVELS):
            m = ch >> (lvl + 1)
            decay = jnp.exp2(expo[n][(lvl + 1) * ch:(lvl + 2) * ch])
            if m % SUBLANES == 0:
                both = jnp.concatenate([(q if blk % 2 else k)[blk * m:(blk + 1) * m]
                                        for blk in range(ch // m)], axis=0)
            else:
                both = jnp.where(((row // m) % 2) == 1, q, k)
            level_sides.append((both * decay).astype(BF16))
        sides.append(level_sides)
        q_bf.append(q.astype(BF16))
        k_bf.append(k.astype(BF16))
        b = expo[n][0:ch]
        b_last = b[ch - 1:ch]
        q_inter.append((q * jnp.exp2(b)).astype(BF16))
        k_state.append((k * jnp.exp2(b_last - b)).astype(BF16))
        state_decay.append(jnp.exp2(b_last))

    heads = [slice(h * HG_DIM, (h + 1) * HG_DIM) for h in range(HG_HEADS)]
    scores = []
    for n in range(HG_STEP_CHUNKS):
        parts = [[lax.dot_general(x[:, sl], y[:, sl], NT_DIMS, preferred_element_type=F32)
                  for x, y in [(q_bf[n], k_bf[n])] + [(sd, sd) for sd in sides[n]]]
                 for sl in heads]
        per_head = []
        for h in range(HG_HEADS):
            sc = parts[h][0] * mask_ref[0]
            for lvl in range(HG_LEVELS):
                sc = sc + parts[h][lvl + 1] * mask_ref[lvl + 1]
            per_head.append(sc.astype(BF16))
        scores.append(per_head)

    intra = [[jnp.dot(scores[n][h], v_all[r, sl], preferred_element_type=F32)
              for h, sl in enumerate(heads)] for n, r in enumerate(chunks)]
    kv = [[lax.dot_general(v_all[r, sl], k_state[n][:, sl], TN_DIMS, preferred_element_type=F32)
           for sl in heads] for n, r in enumerate(chunks)]

    for n, r in enumerate(chunks):
        outs = []
        for h, sl in enumerate(heads):
            st = state_ref[h]
            o_h = intra[n][h] + lax.dot_general(q_inter[n][:, sl], st.astype(BF16), NT_DIMS,
                                                preferred_element_type=F32)
            state_ref[h] = st * state_decay[n][:, sl] + kv[n][h]
            outs.append(o_h * lax.rsqrt(jnp.mean(o_h * o_h, axis=-1, keepdims=True) + RMS_EPS))
        o = jnp.concatenate(outs, axis=1) * ng_ref[...]
        half_gate = 0.5 * gate_ref[0, r, :]
        o_ref[0, r, :] = (o * (half_gate + half_gate * jnp.tanh(half_gate))).astype(o_ref.dtype)


def _hgrn(hq, hf, hi, hgate, lb_logits, hg_norm_g, seq):
    batch = hq.shape[0]
    rows = HG_STEP_CHUNKS * HG_CHUNK
    assert FRONT == rows, "the first grid step must hold exactly the rows in front of x"
    in_spec = pl.BlockSpec((1, rows, HG_WIDTH), lambda b, j: (b, j, 0))
    zmat = _hgrn_exponent_matrix()
    masks = _hgrn_level_masks()
    return pl.pallas_call(
        _hgrn_kernel,
        grid=(batch, (FRONT + seq) // rows),
        in_specs=[in_spec, in_spec, in_spec, in_spec,
                  _const_spec(lb_logits.shape), _const_spec((1, HG_WIDTH)),
                  _const_spec(zmat.shape), _const_spec(masks.shape)],
        out_specs=pl.BlockSpec((1, rows, HG_WIDTH), lambda b, j: (b, jnp.maximum(j - 1, 0), 0)),
        out_shape=jax.ShapeDtypeStruct((batch, seq, HG_WIDTH), BF16),
        scratch_shapes=[pltpu.VMEM((HG_HEADS, HG_DIM, HG_DIM), F32)],
        compiler_params=pltpu.CompilerParams(
            dimension_semantics=("parallel", "arbitrary"), vmem_limit_bytes=VMEM_LIMIT),
        name="hgrn2",
    )(hq, hf, hi, hgate, lb_logits, hg_norm_g, zmat, masks)


def _ffn_kernel(x_ref, osb_ref, ohg_ref, wout_ref, g2_ref, wg_ref, wu_ref, wd_ref, gf_ref, o_ref):
    mix = (jnp.dot(osb_ref[0], wout_ref[0:SB_WIDTH, :], preferred_element_type=F32)
           + jnp.dot(ohg_ref[0], wout_ref[SB_WIDTH:, :], preferred_element_type=F32))
    h1 = x_ref[0] + mix
    u2 = (h1 * lax.rsqrt(jnp.mean(h1 * h1, axis=-1, keepdims=True) + RMS_EPS)
          * g2_ref[...]).astype(BF16)
    ffn = jnp.zeros_like(h1)
    width = D_FF // FFN_SPLIT
    for n in range(FFN_SPLIT):
        cols = slice(n * width, (n + 1) * width)
        gate = jnp.dot(u2, wg_ref[:, cols], preferred_element_type=F32)
        up = jnp.dot(u2, wu_ref[:, cols], preferred_element_type=F32)
        half_gate = 0.5 * gate
        act = ((half_gate + half_gate * jnp.tanh(half_gate)) * up).astype(BF16)
        ffn = ffn + jnp.dot(act, wd_ref[cols, :], preferred_element_type=F32)
    h2 = h1 + ffn
    o_ref[0] = (h2 * lax.rsqrt(jnp.mean(h2 * h2, axis=-1, keepdims=True) + RMS_EPS)
                * gf_ref[...])


def _out_projection_ffn(x, o_sb, o_hg, w_out, norm2_g, w_gate, w_up, w_down, final_g):
    batch, seq, _ = x.shape
    row_spec = lambda width: pl.BlockSpec((1, FFN_ROWS, width), lambda b, j: (b, j, 0))
    resident = lambda shape: pl.BlockSpec(shape, lambda b, j: (0, 0), pipeline_mode=pl.Buffered(1))
    return pl.pallas_call(
        _ffn_kernel,
        grid=(batch, seq // FFN_ROWS),
        in_specs=[row_spec(D_MODEL), row_spec(SB_WIDTH), row_spec(HG_WIDTH),
                  resident((D_MODEL, D_MODEL)), resident((1, D_MODEL)),
                  resident((D_MODEL, D_FF)), resident((D_MODEL, D_FF)), resident((D_FF, D_MODEL)),
                  resident((1, D_MODEL))],
        out_specs=row_spec(D_MODEL),
        out_shape=jax.ShapeDtypeStruct((batch, seq, D_MODEL), x.dtype),
        compiler_params=pltpu.CompilerParams(
            dimension_semantics=("parallel", "parallel"), vmem_limit_bytes=VMEM_LIMIT),
        name="out_projection_ffn",
    )(x, o_sb, o_hg, w_out, norm2_g, w_gate, w_up, w_down, final_g)


def kernel(x, meta_tokens, norm1_g, w_in, sb_norm_g, hg_norm_g, hg_lb_logits, w_out, norm2_g,
           w_gate, w_up, w_down, final_norm_g):
    batch, seq, d_model = x.shape
    assert d_model == D_MODEL and seq % FFN_ROWS == 0 and seq % ATT_TILE == 0
    assert norm1_g.shape[0] == 1, "single-layer block"
    head = jnp.concatenate(
        [jnp.zeros((FIRST_VALID, D_MODEL), x.dtype), meta_tokens.astype(x.dtype)], axis=0)
    row = lambda g: g.reshape(1, -1).astype(F32)

    qkv, hq, hf, hi, hgate = _input_projection(x, head, row(norm1_g[0]), w_in[0].astype(BF16))
    o_sb = _attention(qkv, row(sb_norm_g[0]), seq)
    o_hg = _hgrn(hq, hf, hi, hgate, hg_lb_logits.astype(F32), row(hg_norm_g[0]), seq)
    return _out_projection_ffn(
        x, o_sb, o_hg, w_out[0].astype(BF16), row(norm2_g[0]), w_gate[0].astype(BF16),
        w_up[0].astype(BF16), w_down[0].astype(BF16), row(final_norm_g))
```

```python
import functools

import jax
import jax.numpy as jnp
import numpy as np
from jax import lax
from jax.experimental import pallas as pl
from jax.experimental.pallas import tpu as pltpu

F32 = jnp.float32
BF16 = jnp.bfloat16

D_MODEL = 1024
N_META = 16
SB_HEADS = 8
SB_HEAD_DIM = 64
SB_WIDTH = SB_HEADS * SB_HEAD_DIM
HG_HEADS = 4
HG_DIM = 128
HG_WIDTH = HG_HEADS * HG_DIM
IN_WIDTH = 3 * SB_WIDTH + 4 * HG_WIDTH
D_FF = 2816
RMS_EPS = 1e-6

LANES = 128
SUBLANES = 8
FRONT = 256
FIRST_VALID = FRONT - N_META
ATT_TILE = 256
ATT_PAIRS = 4
HG_CHUNK = 128
HG_LEVELS = 7
HG_STEP_CHUNKS = 2
PROJ_ROWS = 256
FFN_ROWS = 512
FFN_SPLIT = 2
VMEM_LIMIT = 56 * 1024 * 1024
Q_SCALE = float(np.log2(np.e)) * SB_HEAD_DIM ** -0.5
MASKED = -1e9
SP_LINEAR = 64.0
EXIT_SUM = 160.0

NT_DIMS = (((1,), (1,)), ((), ()))
TN_DIMS = (((0,), (0,)), ((), ()))


def _const_spec(shape):
    return pl.BlockSpec(shape, lambda *_: (0,) * len(shape))


def _inproj_kernel(x_ref, head_ref, g_ref, w_ref, qkv_ref, hq_ref, hf_ref, hi_ref, hg_ref):
    j = pl.program_id(1)
    h = jnp.where(j == 0, head_ref[...], x_ref[0])
    u = h * lax.rsqrt(jnp.mean(h * h, axis=-1, keepdims=True) + RMS_EPS) * g_ref[...]
    ub = u.astype(BF16)

    def proj(lo, width):
        return jnp.dot(ub, w_ref[:, lo:lo + width], preferred_element_type=F32)

    qkv_ref[0, :, 0:SB_WIDTH] = (proj(0, SB_WIDTH) * Q_SCALE).astype(BF16)
    qkv_ref[0, :, SB_WIDTH:3 * SB_WIDTH] = proj(SB_WIDTH, 2 * SB_WIDTH).astype(BF16)
    base = 3 * SB_WIDTH
    hq_ref[0] = proj(base, HG_WIDTH).astype(BF16)
    hf_ref[0] = proj(base + HG_WIDTH, HG_WIDTH)
    hi_ref[0] = proj(base + 2 * HG_WIDTH, HG_WIDTH).astype(BF16)
    hg_ref[0] = proj(base + 3 * HG_WIDTH, HG_WIDTH)


def _input_projection(x, head, norm_g, w_in_bf16):
    batch, seq, _ = x.shape
    n_tiles = (FRONT + seq) // PROJ_ROWS
    rows = FRONT + seq
    row_spec = lambda width: pl.BlockSpec((1, PROJ_ROWS, width), lambda b, j: (b, j, 0))
    return pl.pallas_call(
        _inproj_kernel,
        grid=(batch, n_tiles),
        in_specs=[
            pl.BlockSpec((1, PROJ_ROWS, D_MODEL), lambda b, j: (b, jnp.maximum(j - 1, 0), 0)),
            _const_spec((PROJ_ROWS, D_MODEL)),
            _const_spec((1, D_MODEL)),
            _const_spec((D_MODEL, IN_WIDTH)),
        ],
        out_specs=[row_spec(3 * SB_WIDTH), row_spec(HG_WIDTH), row_spec(HG_WIDTH),
                   row_spec(HG_WIDTH), row_spec(HG_WIDTH)],
        out_shape=[
            jax.ShapeDtypeStruct((batch, rows, 3 * SB_WIDTH), BF16),
            jax.ShapeDtypeStruct((batch, rows, HG_WIDTH), BF16),
            jax.ShapeDtypeStruct((batch, rows, HG_WIDTH), F32),
            jax.ShapeDtypeStruct((batch, rows, HG_WIDTH), BF16),
            jax.ShapeDtypeStruct((batch, rows, HG_WIDTH), F32),
        ],
        compiler_params=pltpu.CompilerParams(
            dimension_semantics=("parallel", "parallel"), vmem_limit_bytes=VMEM_LIMIT),
        name="in_projection",
    )(x, head, norm_g, w_in_bf16)


def _attn_kernel(q_ref, k_ref, v_ref, tri_ref, g_ref, o_ref,
                 qh_ref, z_ref, y_ref, c_ref, sum_ref, acc_ref):
    tile = ATT_TILE
    n_chunks = k_ref.shape[1] // tile
    n_heads = z_ref.shape[0]
    i = pl.program_id(2) + FRONT // ATT_TILE
    last = i - FIRST_VALID // ATT_TILE
    lane = lax.broadcasted_iota(jnp.int32, (1, LANES), 1)
    first_head = lane < SB_HEAD_DIM

    def pair_lanes(h):
        return slice((h // 2) * LANES, (h // 2 + 1) * LANES)

    def rows_of(c):
        return pl.ds(pl.multiple_of(c * tile, tile), tile)

    def minus_row(a, row8):
        return (a.reshape(tile // SUBLANES, SUBLANES, tile) - row8[None]).reshape(tile, tile)

    def step(n, masked, stages="ABC"):
        c_new = i - n
        c_a = jnp.maximum(c_new, 0)
        c_c = jnp.clip(c_new + 2, 0, n_chunks - 1)
        if masked:
            s_pos = c_new * tile + lax.broadcasted_iota(jnp.int32, (tile, 1), 0)
            t_pos = i * tile + lax.broadcasted_iota(jnp.int32, (1, tile), 1)
            visible = (s_pos < t_pos) & (s_pos >= FIRST_VALID)
        for h in range(n_heads):
            if "C" in stages:
                local = c_ref[h]
                w = jnp.exp2((y_ref[h] - local).astype(BF16))
                acc_ref[h] += lax.dot_general(v_ref[0, rows_of(c_c), pair_lanes(h)], w, TN_DIMS,
                                              preferred_element_type=F32)
                later = sum_ref[h] + jnp.broadcast_to(local[0:1, :], (SUBLANES, tile))
                sum_ref[h] = later
            if "B" in stages:
                z = z_ref[h]
                if "C" not in stages:
                    later = sum_ref[h]
            if "A" in stages:
                z_new = lax.dot_general(k_ref[0, rows_of(c_a), pair_lanes(h)], qh_ref[h], NT_DIMS,
                                        preferred_element_type=F32)
                if masked:
                    z_new = jnp.where(visible, z_new, MASKED)
                z_ref[h] = z_new
            if "B" in stages:
                zb = z.astype(BF16)
                grown = 1.0 + jnp.exp2(jnp.minimum(zb, SP_LINEAR))
                sp = jnp.maximum(zb, jnp.log2(grown.astype(F32)).astype(BF16))
                y_ref[h] = minus_row(z, later)
                c_ref[h] = jnp.dot(tri_ref[...], sp, preferred_element_type=F32)

    for h in range(n_heads):
        q_pair = q_ref[0, :, pair_lanes(h)]
        qh_ref[h] = jnp.where(first_head == (h % 2 == 0), q_pair, jnp.zeros_like(q_pair))
    sum_ref[...] = jnp.zeros(sum_ref.shape, F32)
    acc_ref[...] = jnp.zeros(acc_ref.shape, F32)
    step(0, True, "A")
    step(1, True, "AB")

    def sweep(carry):
        n, _ = carry
        step(n, False)
        return n + 1, jnp.min(sum_ref[:, 0:1, :] + c_ref[:, 0:1, :])

    n_next, smallest_sum = lax.while_loop(
        lambda carry: (carry[0] < last) & (carry[1] < EXIT_SUM),
        sweep, (jnp.int32(2), jnp.float32(0.0)))

    @pl.when(smallest_sum >= EXIT_SUM)
    def _():
        step(n_next, False, "C")

    @pl.when(smallest_sum < EXIT_SUM)
    def _():
        @pl.when(last > 1)
        def _():
            step(last, True)

        step(last + 1, False, "BC")
        step(last + 2, False, "C")

    for p in range(n_heads // 2):
        o = jnp.concatenate([acc_ref[2 * p, 0:SB_HEAD_DIM], acc_ref[2 * p + 1, SB_HEAD_DIM:LANES]],
                            axis=0).T
        sq = o * o
        ss_first = jnp.sum(jnp.where(first_head, sq, 0.0), axis=-1, keepdims=True)
        ss_second = jnp.sum(sq, axis=-1, keepdims=True) - ss_first
        mean_sq = jnp.where(first_head, ss_first, ss_second) * (1.0 / SB_HEAD_DIM)
        lanes = slice(p * LANES, (p + 1) * LANES)
        o_ref[0, :, lanes] = (o * lax.rsqrt(mean_sq + RMS_EPS) * g_ref[:, lanes]).astype(o_ref.dtype)


def _reverse_cumsum_matrix(n):
    j = np.arange(n)[:, None]
    s = np.arange(n)[None, :]
    return jnp.asarray((s >= j).astype(np.float32), BF16)


def _attention(qkv, sb_norm_g, seq):
    batch, rows, _ = qkv.shape
    width = ATT_PAIRS * LANES
    groups = SB_WIDTH // width
    n_q = seq // ATT_TILE
    heads = 2 * ATT_PAIRS
    return pl.pallas_call(
        _attn_kernel,
        grid=(batch, groups, n_q),
        in_specs=[
            pl.BlockSpec((1, ATT_TILE, width), lambda b, g, i: (b, i + FRONT // ATT_TILE, g)),
            pl.BlockSpec((1, rows, width), lambda b, g, i: (b, 0, groups + g),
                         pipeline_mode=pl.Buffered(1)),
            pl.BlockSpec((1, rows, width), lambda b, g, i: (b, 0, 2 * groups + g),
                         pipeline_mode=pl.Buffered(1)),
            _const_spec((ATT_TILE, ATT_TILE)),
            pl.BlockSpec((1, width), lambda b, g, i: (0, g)),
        ],
        out_specs=pl.BlockSpec((1, ATT_TILE, width), lambda b, g, i: (b, i, g)),
        out_shape=jax.ShapeDtypeStruct((batch, seq, SB_WIDTH), BF16),
        scratch_shapes=[pltpu.VMEM((heads, ATT_TILE, LANES), BF16),
                        pltpu.VMEM((heads, ATT_TILE, ATT_TILE), F32),
                        pltpu.VMEM((heads, ATT_TILE, ATT_TILE), F32),
                        pltpu.VMEM((heads, ATT_TILE, ATT_TILE), F32),
                        pltpu.VMEM((heads, SUBLANES, ATT_TILE), F32),
                        pltpu.VMEM((heads, LANES, ATT_TILE), F32)],
        compiler_params=pltpu.CompilerParams(
            dimension_semantics=("parallel", "parallel", "parallel"), vmem_limit_bytes=VMEM_LIMIT),
        name="stickbreak_attention",
    )(qkv, qkv, qkv, _reverse_cumsum_matrix(ATT_TILE), sb_norm_g)


def _hgrn_exponent_matrix():
    c = HG_CHUNK
    t = np.arange(c)[:, None]
    j = np.arange(c)[None, :]
    mats = [(j <= t)]
    for lvl in range(HG_LEVELS):
        m = c >> (lvl + 1)
        blk = t // m
        odd = (blk % 2) == 1
        prefix = (j >= blk * m) & (j <= t)
        suffix = (j > t) & (j <= blk * m + m - 1)
        mats.append(np.where(odd, prefix, suffix))
    z = np.concatenate(mats, axis=0).astype(np.float32)
    return jnp.asarray(np.concatenate([z, z], axis=1), BF16)


def _hgrn_level_masks():
    c = HG_CHUNK
    t = np.arange(c)[:, None]
    s = np.arange(c)[None, :]
    masks = [t == s]
    for lvl in range(HG_LEVELS):
        m = c >> (lvl + 1)
        masks.append(((t // m) == (s // m) + 1) & (((s // m) % 2) == 0))
    return jnp.asarray(np.stack(masks).astype(np.float32))


def _hgrn_kernel(q_ref, f_ref, i_ref, gate_ref, lbl_ref, ng_ref, zmat_ref, mask_ref, o_ref,
                 state_ref):
    j = pl.program_id(1)
    ch = HG_CHUNK
    rows = HG_STEP_CHUNKS * ch

    @pl.when(j == 0)
    def _():
        state_ref[...] = jnp.zeros_like(state_ref)

    logits = lbl_ref[...]
    mx = jnp.max(logits, axis=0, keepdims=True)
    ex = jnp.exp(logits - mx)
    lb = ex[0:1] / jnp.sum(ex, axis=0, keepdims=True)

    f_raw = f_ref[0]
    sig = 0.5 + 0.5 * jnp.tanh(0.5 * f_raw)
    log_f = jnp.log2(lb + (1.0 - lb) * sig)
    kk = (1.0 - lb) * (1.0 - sig)
    row_all = lax.broadcasted_iota(jnp.int32, (rows, 1), 0)
    valid = (j > 0) | (row_all >= FIRST_VALID)
    log_f = jnp.where(valid, log_f, 0.0)
    kk = jnp.where(valid, kk, 0.0)
    hi = log_f.astype(BF16)
    lo = (log_f - hi.astype(F32)).astype(BF16)
    q_all = q_ref[0].astype(F32)
    v_all = i_ref[0]

    chunks = [slice(n * ch, (n + 1) * ch) for n in range(HG_STEP_CHUNKS)]
    expo = [jnp.dot(zmat_ref[...], jnp.concatenate([hi[r], lo[r]], axis=0),
                    preferred_element_type=F32) for r in chunks]

    row = lax.broadcasted_iota(jnp.int32, (ch, 1), 0)

    sides, q_bf, k_bf, q_inter, k_state, state_decay = [], [], [], [], [], []
    for n, r in enumerate(chunks):
        q, k = q_all[r], kk[r]
        level_sides = []
        for lvl in range(HG_LEVELS):
            m = ch >> (lvl + 1)
            decay = jnp.exp2(expo[n][(lvl + 1) * ch:(lvl + 2) * ch])
            if m % SUBLANES == 0:
                both = jnp.concatenate([(q if blk % 2 else k)[blk * m:(blk + 1) * m]
                                        for blk in range(ch // m)], axis=0)
            else:
                both = jnp.where(((row // m) % 2) == 1, q, k)
            level_sides.append((both * decay).astype(BF16))
        sides.append(level_sides)
        q_bf.append(q.astype(BF16))
        k_bf.append(k.astype(BF16))
        b = expo[n][0:ch]
        b_last = b[ch - 1:ch]
        q_inter.append((q * jnp.exp2(b)).astype(BF16))
        k_state.append((k * jnp.exp2(b_last - b)).astype(BF16))
        state_decay.append(jnp.exp2(b_last))

    heads = [slice(h * HG_DIM, (h + 1) * HG_DIM) for h in range(HG_HEADS)]
    scores = []
    for n in range(HG_STEP_CHUNKS):
        parts = [[lax.dot_general(x[:, sl], y[:, sl], NT_DIMS, preferred_element_type=F32)
                  for x, y in [(q_bf[n], k_bf[n])] + [(sd, sd) for sd in sides[n]]]
                 for sl in heads]
        per_head = []
        for h in range(HG_HEADS):
            sc = parts[h][0] * mask_ref[0]
            for lvl in range(HG_LEVELS):
                sc = sc + parts[h][lvl + 1] * mask_ref[lvl + 1]
            per_head.append(sc.astype(BF16))
        scores.append(per_head)

    intra = [[jnp.dot(scores[n][h], v_all[r, sl], preferred_element_type=F32)
              for h, sl in enumerate(heads)] for n, r in enumerate(chunks)]
    kv = [[lax.dot_general(v_all[r, sl], k_state[n][:, sl], TN_DIMS, preferred_element_type=F32)
           for sl in heads] for n, r in enumerate(chunks)]

    for n, r in enumerate(chunks):
        outs = []
        for h, sl in enumerate(heads):
            st = state_ref[h]
            o_h = intra[n][h] + lax.dot_general(q_inter[n][:, sl], st.astype(BF16), NT_DIMS,
                                                preferred_element_type=F32)
            state_ref[h] = st * state_decay[n][:, sl] + kv[n][h]
            outs.append(o_h * lax.rsqrt(jnp.mean(o_h * o_h, axis=-1, keepdims=True) + RMS_EPS))
        o = jnp.concatenate(outs, axis=1) * ng_ref[...]
        half_gate = 0.5 * gate_ref[0, r, :]
        o_ref[0, r, :] = (o * (half_gate + half_gate * jnp.tanh(half_gate))).astype(o_ref.dtype)


def _hgrn(hq, hf, hi, hgate, lb_logits, hg_norm_g, seq):
    batch = hq.shape[0]
    rows = HG_STEP_CHUNKS * HG_CHUNK
    assert FRONT == rows, "the first grid step must hold exactly the rows in front of x"
    in_spec = pl.BlockSpec((1, rows, HG_WIDTH), lambda b, j: (b, j, 0))
    zmat = _hgrn_exponent_matrix()
    masks = _hgrn_level_masks()
    return pl.pallas_call(
        _hgrn_kernel,
        grid=(batch, (FRONT + seq) // rows),
        in_specs=[in_spec, in_spec, in_spec, in_spec,
                  _const_spec(lb_logits.shape), _const_spec((1, HG_WIDTH)),
                  _const_spec(zmat.shape), _const_spec(masks.shape)],
        out_specs=pl.BlockSpec((1, rows, HG_WIDTH), lambda b, j: (b, jnp.maximum(j - 1, 0), 0)),
        out_shape=jax.ShapeDtypeStruct((batch, seq, HG_WIDTH), BF16),
        scratch_shapes=[pltpu.VMEM((HG_HEADS, HG_DIM, HG_DIM), F32)],
        compiler_params=pltpu.CompilerParams(
            dimension_semantics=("parallel", "arbitrary"), vmem_limit_bytes=VMEM_LIMIT),
        name="hgrn2",
    )(hq, hf, hi, hgate, lb_logits, hg_norm_g, zmat, masks)


def _ffn_kernel(x_ref, osb_ref, ohg_ref, wout_ref, g2_ref, wg_ref, wu_ref, wd_ref, gf_ref, o_ref):
    mix = (jnp.dot(osb_ref[0], wout_ref[0:SB_WIDTH, :], preferred_element_type=F32)
           + jnp.dot(ohg_ref[0], wout_ref[SB_WIDTH:, :], preferred_element_type=F32))
    h1 = x_ref[0] + mix
    u2 = (h1 * lax.rsqrt(jnp.mean(h1 * h1, axis=-1, keepdims=True) + RMS_EPS)
          * g2_ref[...]).astype(BF16)
    ffn = jnp.zeros_like(h1)
    width = D_FF // FFN_SPLIT
    for n in range(FFN_SPLIT):
        cols = slice(n * width, (n + 1) * width)
        gate = jnp.dot(u2, wg_ref[:, cols], preferred_element_type=F32)
        up = jnp.dot(u2, wu_ref[:, cols], preferred_element_type=F32)
        half_gate = 0.5 * gate
        act = ((half_gate + half_gate * jnp.tanh(half_gate)) * up).astype(BF16)
        ffn = ffn + jnp.dot(act, wd_ref[cols, :], preferred_element_type=F32)
    h2 = h1 + ffn
    o_ref[0] = (h2 * lax.rsqrt(jnp.mean(h2 * h2, axis=-1, keepdims=True) + RMS_EPS)
                * gf_ref[...])


def _out_projection_ffn(x, o_sb, o_hg, w_out, norm2_g, w_gate, w_up, w_down, final_g):
    batch, seq, _ = x.shape
    row_spec = lambda width: pl.BlockSpec((1, FFN_ROWS, width), lambda b, j: (b, j, 0))
    resident = lambda shape: pl.BlockSpec(shape, lambda b, j: (0, 0), pipeline_mode=pl.Buffered(1))
    return pl.pallas_call(
        _ffn_kernel,
        grid=(batch, seq // FFN_ROWS),
        in_specs=[row_spec(D_MODEL), row_spec(SB_WIDTH), row_spec(HG_WIDTH),
                  resident((D_MODEL, D_MODEL)), resident((1, D_MODEL)),
                  resident((D_MODEL, D_FF)), resident((D_MODEL, D_FF)), resident((D_FF, D_MODEL)),
                  resident((1, D_MODEL))],
        out_specs=row_spec(D_MODEL),
        out_shape=jax.ShapeDtypeStruct((batch, seq, D_MODEL), x.dtype),
        compiler_params=pltpu.CompilerParams(
            dimension_semantics=("parallel", "parallel"), vmem_limit_bytes=VMEM_LIMIT),
        name="out_projection_ffn",
    )(x, o_sb, o_hg, w_out, norm2_g, w_gate, w_up, w_down, final_g)


def kernel(x, meta_tokens, norm1_g, w_in, sb_norm_g, hg_norm_g, hg_lb_logits, w_out, norm2_g,
           w_gate, w_up, w_down, final_norm_g):
    batch, seq, d_model = x.shape
    assert d_model == D_MODEL and seq % FFN_ROWS == 0 and seq % ATT_TILE == 0
    assert norm1_g.shape[0] == 1, "single-layer block"
    head = jnp.concatenate(
        [jnp.zeros((FIRST_VALID, D_MODEL), x.dtype), meta_tokens.astype(x.dtype)], axis=0)
    row = lambda g: g.reshape(1, -1).astype(F32)

    qkv, hq, hf, hi, hgate = _input_projection(x, head, row(norm1_g[0]), w_in[0].astype(BF16))
    o_sb = _attention(qkv, row(sb_norm_g[0]), seq)
    o_hg = _hgrn(hq, hf, hi, hgate, hg_lb_logits.astype(F32), row(hg_norm_g[0]), seq)
    return _out_projection_ffn(
        x, o_sb, o_hg, w_out[0].astype(BF16), row(norm2_g[0]), w_gate[0].astype(BF16),
        w_up[0].astype(BF16), w_down[0].astype(BF16), row(final_norm_g))
```

```python
import functools

import jax
import jax.numpy as jnp
import numpy as np
from jax import lax
from jax.experimental import pallas as pl
from jax.experimental.pallas import tpu as pltpu

F32 = jnp.float32
BF16 = jnp.bfloat16

D_MODEL = 1024
N_META = 16
SB_HEADS = 8
SB_HEAD_DIM = 64
SB_WIDTH = SB_HEADS * SB_HEAD_DIM
HG_HEADS = 4
HG_DIM = 128
HG_WIDTH = HG_HEADS * HG_DIM
IN_WIDTH = 3 * SB_WIDTH + 4 * HG_WIDTH
D_FF = 2816
RMS_EPS = 1e-6

LANES = 128
SUBLANES = 8
FRONT = 256
FIRST_VALID = FRONT - N_META
ATT_TILE = 256
ATT_PAIRS = 4
HG_CHUNK = 128
HG_LEVELS = 7
HG_STEP_CHUNKS = 2
PROJ_ROWS = 256
FFN_ROWS = 512
FFN_SPLIT = 2
VMEM_LIMIT = 56 * 1024 * 1024
Q_SCALE = float(np.log2(np.e)) * SB_HEAD_DIM ** -0.5
MASKED = -1e9
SP_LINEAR = 64.0
EXIT_SUM = 160.0

NT_DIMS = (((1,), (1,)), ((), ()))
TN_DIMS = (((0,), (0,)), ((), ()))


def _const_spec(shape):
    return pl.BlockSpec(shape, lambda *_: (0,) * len(shape))


def _inproj_kernel(x_ref, head_ref, g_ref, w_ref, qkv_ref, hq_ref, hf_ref, hi_ref, hg_ref):
    j = pl.program_id(1)
    h = jnp.where(j == 0, head_ref[...], x_ref[0])
    u = h * lax.rsqrt(jnp.mean(h * h, axis=-1, keepdims=True) + RMS_EPS) * g_ref[...]
    ub = u.astype(BF16)

    def proj(lo, width):
        return jnp.dot(ub, w_ref[:, lo:lo + width], preferred_element_type=F32)

    qkv_ref[0, :, 0:SB_WIDTH] = (proj(0, SB_WIDTH) * Q_SCALE).astype(BF16)
    qkv_ref[0, :, SB_WIDTH:3 * SB_WIDTH] = proj(SB_WIDTH, 2 * SB_WIDTH).astype(BF16)
    base = 3 * SB_WIDTH
    hq_ref[0] = proj(base, HG_WIDTH).astype(BF16)
    hf_ref[0] = proj(base + HG_WIDTH, HG_WIDTH)
    hi_ref[0] = proj(base + 2 * HG_WIDTH, HG_WIDTH).astype(BF16)
    hg_ref[0] = proj(base + 3 * HG_WIDTH, HG_WIDTH)


def _input_projection(x, head, norm_g, w_in_bf16):
    batch, seq, _ = x.shape
    n_tiles = (FRONT + seq) // PROJ_ROWS
    rows = FRONT + seq
    row_spec = lambda width: pl.BlockSpec((1, PROJ_ROWS, width), lambda b, j: (b, j, 0))
    return pl.pallas_call(
        _inproj_kernel,
        grid=(batch, n_tiles),
        in_specs=[
            pl.BlockSpec((1, PROJ_ROWS, D_MODEL), lambda b, j: (b, jnp.maximum(j - 1, 0), 0)),
            _const_spec((PROJ_ROWS, D_MODEL)),
            _const_spec((1, D_MODEL)),
            _const_spec((D_MODEL, IN_WIDTH)),
        ],
        out_specs=[row_spec(3 * SB_WIDTH), row_spec(HG_WIDTH), row_spec(HG_WIDTH),
                   row_spec(HG_WIDTH), row_spec(HG_WIDTH)],
        out_shape=[
            jax.ShapeDtypeStruct((batch, rows, 3 * SB_WIDTH), BF16),
            jax.ShapeDtypeStruct((batch, rows, HG_WIDTH), BF16),
            jax.ShapeDtypeStruct((batch, rows, HG_WIDTH), F32),
            jax.ShapeDtypeStruct((batch, rows, HG_WIDTH), BF16),
            jax.ShapeDtypeStruct((batch, rows, HG_WIDTH), F32),
        ],
        compiler_params=pltpu.CompilerParams(
            dimension_semantics=("parallel", "parallel"), vmem_limit_bytes=VMEM_LIMIT),
        name="in_projection",
    )(x, head, norm_g, w_in_bf16)


def _attn_kernel(q_ref, k_ref, v_ref, tri_ref, g_ref, o_ref, qh_ref, z_ref, c_ref, sum_ref, acc_ref):
    tile = ATT_TILE
    n_heads = qh_ref.shape[0]
    i = pl.program_id(2) + FRONT // ATT_TILE
    last = i - FIRST_VALID // ATT_TILE
    lane = lax.broadcasted_iota(jnp.int32, (1, LANES), 1)
    first_head = lane < SB_HEAD_DIM
    t_pos = i * tile + lax.broadcasted_iota(jnp.int32, (1, tile), 1)

    def pair_lanes(h):
        return slice((h // 2) * LANES, (h // 2 + 1) * LANES)

    def rows_of(c):
        return pl.ds(pl.multiple_of(c * tile, tile), tile)

    def minus_row(a, row8):
        return (a.reshape(tile // SUBLANES, SUBLANES, tile) - row8[None]).reshape(tile, tile)

    def process(positions):
        for slot, n in enumerate(positions):
            s_pos = (i - n) * tile + lax.broadcasted_iota(jnp.int32, (tile, 1), 0)
            visible = (s_pos < t_pos) & (s_pos >= FIRST_VALID)
            for h in range(n_heads):
                z = lax.dot_general(k_ref[0, rows_of(i - n), pair_lanes(h)], qh_ref[h], NT_DIMS,
                                    preferred_element_type=F32)
                z_ref[slot, h] = jnp.where(visible, z, MASKED)
        for slot in range(len(positions)):
            for h in range(n_heads):
                zb = z_ref[slot, h].astype(BF16)
                grown = 1.0 + jnp.exp2(jnp.minimum(zb, SP_LINEAR))
                sp = jnp.maximum(zb, jnp.log2(grown.astype(F32)).astype(BF16))
                c_ref[slot, h] = jnp.dot(tri_ref[...], sp, preferred_element_type=F32)
        for h in range(n_heads):
            later = sum_ref[h]
            acc = acc_ref[h]
            for slot, n in enumerate(positions):
                local = c_ref[slot, h]
                w = jnp.exp2((minus_row(z_ref[slot, h], later) - local).astype(BF16))
                acc = acc + lax.dot_general(v_ref[0, rows_of(i - n), pair_lanes(h)], w, TN_DIMS,
                                            preferred_element_type=F32)
                later = later + jnp.broadcast_to(local[0:1, :], (SUBLANES, tile))
            sum_ref[h] = later
            acc_ref[h] = acc

    for h in range(n_heads):
        q_pair = q_ref[0, :, pair_lanes(h)]
        qh_ref[h] = jnp.where(first_head == (h % 2 == 0), q_pair, jnp.zeros_like(q_pair))
    sum_ref[...] = jnp.zeros(sum_ref.shape, F32)
    acc_ref[...] = jnp.zeros(acc_ref.shape, F32)
    process([0, 1])

    def sweep(carry):
        n, _ = carry
        process([n])
        return n + 1, jnp.min(sum_ref[...])

    lax.while_loop(lambda carry: (carry[0] <= last) & (carry[1] < EXIT_SUM),
                   sweep, (jnp.int32(2), jnp.min(sum_ref[...])))

    for p in range(n_heads // 2):
        o = jnp.concatenate([acc_ref[2 * p, 0:SB_HEAD_DIM], acc_ref[2 * p + 1, SB_HEAD_DIM:LANES]],
                            axis=0).T
        sq = o * o
        ss_first = jnp.sum(jnp.where(first_head, sq, 0.0), axis=-1, keepdims=True)
        ss_second = jnp.sum(sq, axis=-1, keepdims=True) - ss_first
        mean_sq = jnp.where(first_head, ss_first, ss_second) * (1.0 / SB_HEAD_DIM)
        lanes = slice(p * LANES, (p + 1) * LANES)
        o_ref[0, :, lanes] = (o * lax.rsqrt(mean_sq + RMS_EPS) * g_ref[:, lanes]).astype(o_ref.dtype)


def _reverse_cumsum_matrix(n):
    j = np.arange(n)[:, None]
    s = np.arange(n)[None, :]
    return jnp.asarray((s >= j).astype(np.float32), BF16)


def _attention(qkv, sb_norm_g, seq):
    batch, rows, _ = qkv.shape
    width = ATT_PAIRS * LANES
    groups = SB_WIDTH // width
    n_q = seq // ATT_TILE
    heads = 2 * ATT_PAIRS
    return pl.pallas_call(
        _attn_kernel,
        grid=(batch, groups, n_q),
        in_specs=[
            pl.BlockSpec((1, ATT_TILE, width), lambda b, g, i: (b, i + FRONT // ATT_TILE, g)),
            pl.BlockSpec((1, rows, width), lambda b, g, i: (b, 0, groups + g),
                         pipeline_mode=pl.Buffered(1)),
            pl.BlockSpec((1, rows, width), lambda b, g, i: (b, 0, 2 * groups + g),
                         pipeline_mode=pl.Buffered(1)),
            _const_spec((ATT_TILE, ATT_TILE)),
            pl.BlockSpec((1, width), lambda b, g, i: (0, g)),
        ],
        out_specs=pl.BlockSpec((1, ATT_TILE, width), lambda b, g, i: (b, i, g)),
        out_shape=jax.ShapeDtypeStruct((batch, seq, SB_WIDTH), BF16),
        scratch_shapes=[pltpu.VMEM((heads, ATT_TILE, LANES), BF16),
                        pltpu.VMEM((2, heads, ATT_TILE, ATT_TILE), F32),
                        pltpu.VMEM((2, heads, ATT_TILE, ATT_TILE), F32),
                        pltpu.VMEM((heads, SUBLANES, ATT_TILE), F32),
                        pltpu.VMEM((heads, LANES, ATT_TILE), F32)],
        compiler_params=pltpu.CompilerParams(
            dimension_semantics=("parallel", "parallel", "parallel"), vmem_limit_bytes=VMEM_LIMIT),
        name="stickbreak_attention",
    )(qkv, qkv, qkv, _reverse_cumsum_matrix(ATT_TILE), sb_norm_g)


def _hgrn_exponent_matrix():
    c = HG_CHUNK
    t = np.arange(c)[:, None]
    j = np.arange(c)[None, :]
    mats = [(j <= t)]
    for lvl in range(HG_LEVELS):
        m = c >> (lvl + 1)
        blk = t // m
        odd = (blk % 2) == 1
        prefix = (j >= blk * m) & (j <= t)
        suffix = (j > t) & (j <= blk * m + m - 1)
        mats.append(np.where(odd, prefix, suffix))
    z = np.concatenate(mats, axis=0).astype(np.float32)
    return jnp.asarray(np.concatenate([z, z], axis=1), BF16)


def _hgrn_level_masks():
    c = HG_CHUNK
    t = np.arange(c)[:, None]
    s = np.arange(c)[None, :]
    masks = [t == s]
    for lvl in range(HG_LEVELS):
        m = c >> (lvl + 1)
        masks.append(((t // m) == (s // m) + 1) & (((s // m) % 2) == 0))
    return jnp.asarray(np.stack(masks).astype(np.float32))


def _hgrn_kernel(q_ref, f_ref, i_ref, gate_ref, lbl_ref, ng_ref, zmat_ref, mask_ref, o_ref,
                 state_ref):
    j = pl.program_id(1)
    ch = HG_CHUNK
    rows = HG_STEP_CHUNKS * ch

    @pl.when(j == 0)
    def _():
        state_ref[...] = jnp.zeros_like(state_ref)

    logits = lbl_ref[...]
    mx = jnp.max(logits, axis=0, keepdims=True)
    ex = jnp.exp(logits - mx)
    lb = ex[0:1] / jnp.sum(ex, axis=0, keepdims=True)

    f_raw = f_ref[0]
    sig = 0.5 + 0.5 * jnp.tanh(0.5 * f_raw)
    log_f = jnp.log2(lb + (1.0 - lb) * sig)
    kk = (1.0 - lb) * (1.0 - sig)
    row_all = lax.broadcasted_iota(jnp.int32, (rows, 1), 0)
    valid = (j > 0) | (row_all >= FIRST_VALID)
    log_f = jnp.where(valid, log_f, 0.0)
    kk = jnp.where(valid, kk, 0.0)
    hi = log_f.astype(BF16)
    lo = (log_f - hi.astype(F32)).astype(BF16)
    q_all = q_ref[0].astype(F32)
    v_all = i_ref[0]

    chunks = [slice(n * ch, (n + 1) * ch) for n in range(HG_STEP_CHUNKS)]
    expo = [jnp.dot(zmat_ref[...], jnp.concatenate([hi[r], lo[r]], axis=0),
                    preferred_element_type=F32) for r in chunks]

    row = lax.broadcasted_iota(jnp.int32, (ch, 1), 0)

    sides, q_bf, k_bf, q_inter, k_state, state_decay = [], [], [], [], [], []
    for n, r in enumerate(chunks):
        q, k = q_all[r], kk[r]
        level_sides = []
        for lvl in range(HG_LEVELS):
            m = ch >> (lvl + 1)
            decay = jnp.exp2(expo[n][(lvl + 1) * ch:(lvl + 2) * ch])
            if m % SUBLANES == 0:
                both = jnp.concatenate([(q if blk % 2 else k)[blk * m:(blk + 1) * m]
                                        for blk in range(ch // m)], axis=0)
            else:
                both = jnp.where(((row // m) % 2) == 1, q, k)
            level_sides.append((both * decay).astype(BF16))
        sides.append(level_sides)
        q_bf.append(q.astype(BF16))
        k_bf.append(k.astype(BF16))
        b = expo[n][0:ch]
        b_last = b[ch - 1:ch]
        q_inter.append((q * jnp.exp2(b)).astype(BF16))
        k_state.append((k * jnp.exp2(b_last - b)).astype(BF16))
        state_decay.append(jnp.exp2(b_last))

    heads = [slice(h * HG_DIM, (h + 1) * HG_DIM) for h in range(HG_HEADS)]
    scores = []
    for n in range(HG_STEP_CHUNKS):
        parts = [[lax.dot_general(x[:, sl], y[:, sl], NT_DIMS, preferred_element_type=F32)
                  for x, y in [(q_bf[n], k_bf[n])] + [(sd, sd) for sd in sides[n]]]
                 for sl in heads]
        per_head = []
        for h in range(HG_HEADS):
            sc = parts[h][0] * mask_ref[0]
            for lvl in range(HG_LEVELS):
                sc = sc + parts[h][lvl + 1] * mask_ref[lvl + 1]
            per_head.append(sc.astype(BF16))
        scores.append(per_head)

    intra = [[jnp.dot(scores[n][h], v_all[r, sl], preferred_element_type=F32)
              for h, sl in enumerate(heads)] for n, r in enumerate(chunks)]
    kv = [[lax.dot_general(v_all[r, sl], k_state[n][:, sl], TN_DIMS, preferred_element_type=F32)
           for sl in heads] for n, r in enumerate(chunks)]

    for n, r in enumerate(chunks):
        outs = []
        for h, sl in enumerate(heads):
            st = state_ref[h]
            o_h = intra[n][h] + lax.dot_general(q_inter[n][:, sl], st.astype(BF16), NT_DIMS,
                                                preferred_element_type=F32)
            state_ref[h] = st * state_decay[n][:, sl] + kv[n][h]
            outs.append(o_h * lax.rsqrt(jnp.mean(o_h * o_h, axis=-1, keepdims=True) + RMS_EPS))
        o = jnp.concatenate(outs, axis=1) * ng_ref[...]
        half_gate = 0.5 * gate_ref[0, r, :]
        o_ref[0, r, :] = (o * (half_gate + half_gate * jnp.tanh(half_gate))).astype(o_ref.dtype)


def _hgrn(hq, hf, hi, hgate, lb_logits, hg_norm_g, seq):
    batch = hq.shape[0]
    rows = HG_STEP_CHUNKS * HG_CHUNK
    assert FRONT == rows, "the first grid step must hold exactly the rows in front of x"
    in_spec = pl.BlockSpec((1, rows, HG_WIDTH), lambda b, j: (b, j, 0))
    zmat = _hgrn_exponent_matrix()
    masks = _hgrn_level_masks()
    return pl.pallas_call(
        _hgrn_kernel,
        grid=(batch, (FRONT + seq) // rows),
        in_specs=[in_spec, in_spec, in_spec, in_spec,
                  _const_spec(lb_logits.shape), _const_spec((1, HG_WIDTH)),
                  _const_spec(zmat.shape), _const_spec(masks.shape)],
        out_specs=pl.BlockSpec((1, rows, HG_WIDTH), lambda b, j: (b, jnp.maximum(j - 1, 0), 0)),
        out_shape=jax.ShapeDtypeStruct((batch, seq, HG_WIDTH), BF16),
        scratch_shapes=[pltpu.VMEM((HG_HEADS, HG_DIM, HG_DIM), F32)],
        compiler_params=pltpu.CompilerParams(
            dimension_semantics=("parallel", "arbitrary"), vmem_limit_bytes=VMEM_LIMIT),
        name="hgrn2",
    )(hq, hf, hi, hgate, lb_logits, hg_norm_g, zmat, masks)


def _ffn_kernel(x_ref, osb_ref, ohg_ref, wout_ref, g2_ref, wg_ref, wu_ref, wd_ref, gf_ref, o_ref):
    mix = (jnp.dot(osb_ref[0], wout_ref[0:SB_WIDTH, :], preferred_element_type=F32)
           + jnp.dot(ohg_ref[0], wout_ref[SB_WIDTH:, :], preferred_element_type=F32))
    h1 = x_ref[0] + mix
    u2 = (h1 * lax.rsqrt(jnp.mean(h1 * h1, axis=-1, keepdims=True) + RMS_EPS)
          * g2_ref[...]).astype(BF16)
    ffn = jnp.zeros_like(h1)
    width = D_FF // FFN_SPLIT
    for n in range(FFN_SPLIT):
        cols = slice(n * width, (n + 1) * width)
        gate = jnp.dot(u2, wg_ref[:, cols], preferred_element_type=F32)
        up = jnp.dot(u2, wu_ref[:, cols], preferred_element_type=F32)
        half_gate = 0.5 * gate
        act = ((half_gate + half_gate * jnp.tanh(half_gate)) * up).astype(BF16)
        ffn = ffn + jnp.dot(act, wd_ref[cols, :], preferred_element_type=F32)
    h2 = h1 + ffn
    o_ref[0] = (h2 * lax.rsqrt(jnp.mean(h2 * h2, axis=-1, keepdims=True) + RMS_EPS)
                * gf_ref[...])


def _out_projection_ffn(x, o_sb, o_hg, w_out, norm2_g, w_gate, w_up, w_down, final_g):
    batch, seq, _ = x.shape
    row_spec = lambda width: pl.BlockSpec((1, FFN_ROWS, width), lambda b, j: (b, j, 0))
    resident = lambda shape: pl.BlockSpec(shape, lambda b, j: (0, 0), pipeline_mode=pl.Buffered(1))
    return pl.pallas_call(
        _ffn_kernel,
        grid=(batch, seq // FFN_ROWS),
        in_specs=[row_spec(D_MODEL), row_spec(SB_WIDTH), row_spec(HG_WIDTH),
                  resident((D_MODEL, D_MODEL)), resident((1, D_MODEL)),
                  resident((D_MODEL, D_FF)), resident((D_MODEL, D_FF)), resident((D_FF, D_MODEL)),
                  resident((1, D_MODEL))],
        out_specs=row_spec(D_MODEL),
        out_shape=jax.ShapeDtypeStruct((batch, seq, D_MODEL), x.dtype),
        compiler_params=pltpu.CompilerParams(
            dimension_semantics=("parallel", "parallel"), vmem_limit_bytes=VMEM_LIMIT),
        name="out_projection_ffn",
    )(x, o_sb, o_hg, w_out, norm2_g, w_gate, w_up, w_down, final_g)


def kernel(x, meta_tokens, norm1_g, w_in, sb_norm_g, hg_norm_g, hg_lb_logits, w_out, norm2_g,
           w_gate, w_up, w_down, final_norm_g):
    batch, seq, d_model = x.shape
    assert d_model == D_MODEL and seq % FFN_ROWS == 0 and seq % ATT_TILE == 0
    assert norm1_g.shape[0] == 1, "single-layer block"
    head = jnp.concatenate(
        [jnp.zeros((FIRST_VALID, D_MODEL), x.dtype), meta_tokens.astype(x.dtype)], axis=0)
    row = lambda g: g.reshape(1, -1).astype(F32)

    qkv, hq, hf, hi, hgate = _input_projection(x, head, row(norm1_g[0]), w_in[0].astype(BF16))
    o_sb = _attention(qkv, row(sb_norm_g[0]), seq)
    o_hg = _hgrn(hq, hf, hi, hgate, hg_lb_logits.astype(F32), row(hg_norm_g[0]), seq)
    return _out_projection_ffn(
        x, o_sb, o_hg, w_out[0].astype(BF16), row(norm2_g[0]), w_gate[0].astype(BF16),
        w_up[0].astype(BF16), w_down[0].astype(BF16), row(final_norm_g))
```

```python
import functools

import jax
import jax.numpy as jnp
import numpy as np
from jax import lax
from jax.experimental import pallas as pl
from jax.experimental.pallas import tpu as pltpu

F32 = jnp.float32
BF16 = jnp.bfloat16

D_MODEL = 1024
N_META = 16
SB_HEADS = 8
SB_HEAD_DIM = 64
SB_WIDTH = SB_HEADS * SB_HEAD_DIM
HG_HEADS = 4
HG_DIM = 128
HG_WIDTH = HG_HEADS * HG_DIM
IN_WIDTH = 3 * SB_WIDTH + 4 * HG_WIDTH
D_FF = 2816
RMS_EPS = 1e-6

LANES = 128
SUBLANES = 8
FRONT = 256
FIRST_VALID = FRONT - N_META
ATT_TILE = 256
ATT_PAIRS = 4
HG_CHUNK = 128
HG_LEVELS = 7
HG_STEP_CHUNKS = 2
PROJ_ROWS = 256
FFN_ROWS = 512
FFN_SPLIT = 2
VMEM_LIMIT = 56 * 1024 * 1024
Q_SCALE = float(np.log2(np.e)) * SB_HEAD_DIM ** -0.5
MASKED = -1e9
SP_LINEAR = 64.0
EXIT_SUM = 160.0

NT_DIMS = (((1,), (1,)), ((), ()))
TN_DIMS = (((0,), (0,)), ((), ()))


def _const_spec(shape):
    return pl.BlockSpec(shape, lambda *_: (0,) * len(shape))


def _inproj_kernel(x_ref, head_ref, g_ref, w32_ref, qkv_ref, hq_ref, hf_ref, hi_ref, hg_ref, w_ref):
    j = pl.program_id(1)

    @pl.when(j == 0)
    def _():
        w_ref[...] = w32_ref[...].astype(BF16)

    h = jnp.where(j == 0, head_ref[...], x_ref[0])
    u = h * lax.rsqrt(jnp.mean(h * h, axis=-1, keepdims=True) + RMS_EPS) * g_ref[...]
    ub = u.astype(BF16)

    def proj(lo, width):
        return jnp.dot(ub, w_ref[:, lo:lo + width], preferred_element_type=F32)

    qkv_ref[0, :, 0:SB_WIDTH] = (proj(0, SB_WIDTH) * Q_SCALE).astype(BF16)
    qkv_ref[0, :, SB_WIDTH:3 * SB_WIDTH] = proj(SB_WIDTH, 2 * SB_WIDTH).astype(BF16)
    base = 3 * SB_WIDTH
    hq_ref[0] = proj(base, HG_WIDTH).astype(BF16)
    hf_ref[0] = proj(base + HG_WIDTH, HG_WIDTH)
    hi_ref[0] = proj(base + 2 * HG_WIDTH, HG_WIDTH).astype(BF16)
    hg_ref[0] = proj(base + 3 * HG_WIDTH, HG_WIDTH)


def _input_projection(x, head, norm_g, w_in):
    batch, seq, _ = x.shape
    n_tiles = (FRONT + seq) // PROJ_ROWS
    rows = FRONT + seq
    row_spec = lambda width: pl.BlockSpec((1, PROJ_ROWS, width), lambda b, j: (b, j, 0))
    return pl.pallas_call(
        _inproj_kernel,
        grid=(batch, n_tiles),
        in_specs=[
            pl.BlockSpec((1, PROJ_ROWS, D_MODEL), lambda b, j: (b, jnp.maximum(j - 1, 0), 0)),
            _const_spec((PROJ_ROWS, D_MODEL)),
            _const_spec((1, D_MODEL)),
            pl.BlockSpec((D_MODEL, IN_WIDTH), lambda b, j: (0, 0), pipeline_mode=pl.Buffered(1)),
        ],
        out_specs=[row_spec(3 * SB_WIDTH), row_spec(HG_WIDTH), row_spec(HG_WIDTH),
                   row_spec(HG_WIDTH), row_spec(HG_WIDTH)],
        out_shape=[
            jax.ShapeDtypeStruct((batch, rows, 3 * SB_WIDTH), BF16),
            jax.ShapeDtypeStruct((batch, rows, HG_WIDTH), BF16),
            jax.ShapeDtypeStruct((batch, rows, HG_WIDTH), F32),
            jax.ShapeDtypeStruct((batch, rows, HG_WIDTH), BF16),
            jax.ShapeDtypeStruct((batch, rows, HG_WIDTH), F32),
        ],
        scratch_shapes=[pltpu.VMEM((D_MODEL, IN_WIDTH), BF16)],
        compiler_params=pltpu.CompilerParams(
            dimension_semantics=("arbitrary", "arbitrary"), vmem_limit_bytes=VMEM_LIMIT),
        name="in_projection",
    )(x, head, norm_g, w_in)


def _attn_kernel(q_ref, k_ref, v_ref, tri_ref, g_ref, o_ref, qh_ref, z_ref, c_ref, sum_ref, acc_ref):
    tile = ATT_TILE
    n_heads = qh_ref.shape[0]
    i = pl.program_id(2) + FRONT // ATT_TILE
    last = i - FIRST_VALID // ATT_TILE
    lane = lax.broadcasted_iota(jnp.int32, (1, LANES), 1)
    first_head = lane < SB_HEAD_DIM
    t_pos = i * tile + lax.broadcasted_iota(jnp.int32, (1, tile), 1)

    def pair_lanes(h):
        return slice((h // 2) * LANES, (h // 2 + 1) * LANES)

    def rows_of(c):
        return pl.ds(pl.multiple_of(c * tile, tile), tile)

    def minus_row(a, row8):
        return (a.reshape(tile // SUBLANES, SUBLANES, tile) - row8[None]).reshape(tile, tile)

    def process(positions, fresh=False):
        visible = []
        for n in positions:
            s_pos = (i - n) * tile + lax.broadcasted_iota(jnp.int32, (tile, 1), 0)
            visible.append((s_pos < t_pos) & (s_pos >= FIRST_VALID))

        def scores(h):
            for slot, n in enumerate(positions):
                z = lax.dot_general(k_ref[0, rows_of(i - n), pair_lanes(h)], qh_ref[h], NT_DIMS,
                                    preferred_element_type=F32)
                z_ref[slot, h] = jnp.where(visible[slot], z, MASKED)

        def suffix_sums(h):
            for slot in range(len(positions)):
                zb = z_ref[slot, h].astype(BF16)
                grown = 1.0 + jnp.exp2(jnp.minimum(zb, SP_LINEAR))
                sp = jnp.maximum(zb, jnp.log2(grown.astype(F32)).astype(BF16))
                c_ref[slot, h] = jnp.dot(tri_ref[...], sp, preferred_element_type=F32)

        def weights(h):
            later = sum_ref[h]
            acc = acc_ref[h]
            for slot, n in enumerate(positions):
                local = c_ref[slot, h]
                z = z_ref[slot, h] if fresh and slot == 0 else minus_row(z_ref[slot, h], later)
                w = jnp.exp2((z - local).astype(BF16))
                acc = acc + lax.dot_general(v_ref[0, rows_of(i - n), pair_lanes(h)], w, TN_DIMS,
                                            preferred_element_type=F32)
                later = later + jnp.broadcast_to(local[0:1, :], (SUBLANES, tile))
            sum_ref[h] = later
            acc_ref[h] = acc

        phases = (scores, suffix_sums, weights)
        if len(positions) == 1:
            for phase in phases:
                for h in range(n_heads):
                    phase(h)
            return
        for k in range(n_heads + len(phases) - 1):
            for lag, phase in enumerate(phases):
                if 0 <= k - lag < n_heads:
                    phase(k - lag)

    for h in range(n_heads):
        q_pair = q_ref[0, :, pair_lanes(h)]
        qh_ref[h] = jnp.where(first_head == (h % 2 == 0), q_pair, jnp.zeros_like(q_pair))
    sum_ref[...] = jnp.zeros(sum_ref.shape, F32)
    acc_ref[...] = jnp.zeros(acc_ref.shape, F32)
    process([0, 1], fresh=True)

    def sweep(carry):
        n, _ = carry
        process([n])
        return n + 1, jnp.min(sum_ref[...])

    lax.while_loop(lambda carry: (carry[0] <= last) & (carry[1] < EXIT_SUM),
                   sweep, (jnp.int32(2), jnp.min(sum_ref[...])))

    for p in range(n_heads // 2):
        o = jnp.concatenate([acc_ref[2 * p, 0:SB_HEAD_DIM], acc_ref[2 * p + 1, SB_HEAD_DIM:LANES]],
                            axis=0).T
        sq = o * o
        ss_first = jnp.sum(jnp.where(first_head, sq, 0.0), axis=-1, keepdims=True)
        ss_second = jnp.sum(sq, axis=-1, keepdims=True) - ss_first
        mean_sq = jnp.where(first_head, ss_first, ss_second) * (1.0 / SB_HEAD_DIM)
        lanes = slice(p * LANES, (p + 1) * LANES)
        o_ref[0, :, lanes] = (o * lax.rsqrt(mean_sq + RMS_EPS) * g_ref[:, lanes]).astype(o_ref.dtype)


def _reverse_cumsum_matrix(n):
    j = np.arange(n)[:, None]
    s = np.arange(n)[None, :]
    return jnp.asarray((s >= j).astype(np.float32), BF16)


def _attention(qkv, sb_norm_g, seq):
    batch, rows, _ = qkv.shape
    width = ATT_PAIRS * LANES
    groups = SB_WIDTH // width
    n_q = seq // ATT_TILE
    heads = 2 * ATT_PAIRS
    return pl.pallas_call(
        _attn_kernel,
        grid=(batch, groups, n_q),
        in_specs=[
            pl.BlockSpec((1, ATT_TILE, width), lambda b, g, i: (b, i + FRONT // ATT_TILE, g)),
            pl.BlockSpec((1, rows, width), lambda b, g, i: (b, 0, groups + g),
                         pipeline_mode=pl.Buffered(1)),
            pl.BlockSpec((1, rows, width), lambda b, g, i: (b, 0, 2 * groups + g),
                         pipeline_mode=pl.Buffered(1)),
            _const_spec((ATT_TILE, ATT_TILE)),
            pl.BlockSpec((1, width), lambda b, g, i: (0, g)),
        ],
        out_specs=pl.BlockSpec((1, ATT_TILE, width), lambda b, g, i: (b, i, g)),
        out_shape=jax.ShapeDtypeStruct((batch, seq, SB_WIDTH), BF16),
        scratch_shapes=[pltpu.VMEM((heads, ATT_TILE, LANES), BF16),
                        pltpu.VMEM((2, heads, ATT_TILE, ATT_TILE), F32),
                        pltpu.VMEM((2, heads, ATT_TILE, ATT_TILE), F32),
                        pltpu.VMEM((heads, SUBLANES, ATT_TILE), F32),
                        pltpu.VMEM((heads, LANES, ATT_TILE), F32)],
        compiler_params=pltpu.CompilerParams(
            dimension_semantics=("parallel", "parallel", "parallel"), vmem_limit_bytes=VMEM_LIMIT),
        name="stickbreak_attention",
    )(qkv, qkv, qkv, _reverse_cumsum_matrix(ATT_TILE), sb_norm_g)


def _hgrn_exponent_matrix():
    c = HG_CHUNK
    t = np.arange(c)[:, None]
    j = np.arange(c)[None, :]
    mats = [(j <= t)]
    for lvl in range(HG_LEVELS):
        m = c >> (lvl + 1)
        blk = t // m
        odd = (blk % 2) == 1
        prefix = (j >= blk * m) & (j <= t)
        suffix = (j > t) & (j <= blk * m + m - 1)
        mats.append(np.where(odd, prefix, suffix))
    z = np.concatenate(mats, axis=0).astype(np.float32)
    return jnp.asarray(np.concatenate([z, z], axis=1), BF16)


def _hgrn_level_masks():
    c = HG_CHUNK
    t = np.arange(c)[:, None]
    s = np.arange(c)[None, :]
    masks = [t == s]
    for lvl in range(HG_LEVELS):
        m = c >> (lvl + 1)
        masks.append(((t // m) == (s // m) + 1) & (((s // m) % 2) == 0))
    return jnp.asarray(np.stack(masks).astype(np.float32))


def _hgrn_kernel(q_ref, f_ref, i_ref, gate_ref, lbl_ref, ng_ref, zmat_ref, mask_ref, o_ref,
                 state_ref):
    j = pl.program_id(1)
    ch = HG_CHUNK
    rows = HG_STEP_CHUNKS * ch

    @pl.when(j == 0)
    def _():
        state_ref[...] = jnp.zeros_like(state_ref)

    logits = lbl_ref[...]
    mx = jnp.max(logits, axis=0, keepdims=True)
    ex = jnp.exp(logits - mx)
    lb = ex[0:1] / jnp.sum(ex, axis=0, keepdims=True)

    f_raw = f_ref[0]
    sig = 0.5 + 0.5 * jnp.tanh(0.5 * f_raw)
    log_f = jnp.log2(lb + (1.0 - lb) * sig)
    kk = (1.0 - lb) * (1.0 - sig)
    row_all = lax.broadcasted_iota(jnp.int32, (rows, 1), 0)
    valid = (j > 0) | (row_all >= FIRST_VALID)
    log_f = jnp.where(valid, log_f, 0.0)
    kk = jnp.where(valid, kk, 0.0)
    hi = log_f.astype(BF16)
    lo = (log_f - hi.astype(F32)).astype(BF16)
    q_all = q_ref[0].astype(F32)
    v_all = i_ref[0]

    chunks = [slice(n * ch, (n + 1) * ch) for n in range(HG_STEP_CHUNKS)]
    heads = [slice(h * HG_DIM, (h + 1) * HG_DIM) for h in range(HG_HEADS)]
    row = lax.broadcasted_iota(jnp.int32, (ch, 1), 0)
    expo, sides, q_bf, k_bf, q_inter, k_state, state_decay = {}, {}, {}, {}, {}, {}, {}
    parts, scores, intra, kv = {}, {}, {}, {}

    def exponents(n):
        r = chunks[n]
        expo[n] = jnp.dot(zmat_ref[...], jnp.concatenate([hi[r], lo[r]], axis=0),
                          preferred_element_type=F32)

    def decayed_operands(n):
        q, k = q_all[chunks[n]], kk[chunks[n]]
        level_sides = []
        for lvl in range(HG_LEVELS):
            m = ch >> (lvl + 1)
            decay = jnp.exp2(expo[n][(lvl + 1) * ch:(lvl + 2) * ch])
            if m % SUBLANES == 0:
                both = jnp.concatenate([(q if blk % 2 else k)[blk * m:(blk + 1) * m]
                                        for blk in range(ch // m)], axis=0)
            else:
                both = jnp.where(((row // m) % 2) == 1, q, k)
            level_sides.append((both * decay).astype(BF16))
        sides[n] = level_sides
        q_bf[n] = q.astype(BF16)
        k_bf[n] = k.astype(BF16)
        b = expo[n][0:ch]
        b_last = b[ch - 1:ch]
        q_inter[n] = (q * jnp.exp2(b)).astype(BF16)
        k_state[n] = (k * jnp.exp2(b_last - b)).astype(BF16)
        state_decay[n] = jnp.exp2(b_last)

    def level_products(n):
        parts[n] = [[lax.dot_general(x[:, sl], y[:, sl], NT_DIMS, preferred_element_type=F32)
                     for x, y in [(q_bf[n], k_bf[n])] + [(sd, sd) for sd in sides[n]]]
                    for sl in heads]

    def assemble_scores(n):
        per_head = []
        for h in range(HG_HEADS):
            sc = parts[n][h][0] * mask_ref[0]
            for lvl in range(HG_LEVELS):
                sc = sc + parts[n][h][lvl + 1] * mask_ref[lvl + 1]
            per_head.append(sc.astype(BF16))
        scores[n] = per_head

    def value_products(n):
        r = chunks[n]
        intra[n] = [jnp.dot(scores[n][h], v_all[r, sl], preferred_element_type=F32)
                    for h, sl in enumerate(heads)]
        kv[n] = [lax.dot_general(v_all[r, sl], k_state[n][:, sl], TN_DIMS,
                                 preferred_element_type=F32) for sl in heads]

    phases = (exponents, decayed_operands, level_products, assemble_scores, value_products)
    for k in range(HG_STEP_CHUNKS + len(phases) - 1):
        for lag, phase in enumerate(phases):
            if 0 <= k - lag < HG_STEP_CHUNKS:
                phase(k - lag)

    for n, r in enumerate(chunks):
        outs = []
        for h, sl in enumerate(heads):
            st = state_ref[h]
            o_h = intra[n][h] + lax.dot_general(q_inter[n][:, sl], st.astype(BF16), NT_DIMS,
                                                preferred_element_type=F32)
            state_ref[h] = st * state_decay[n][:, sl] + kv[n][h]
            outs.append(o_h * lax.rsqrt(jnp.mean(o_h * o_h, axis=-1, keepdims=True) + RMS_EPS))
        o = jnp.concatenate(outs, axis=1) * ng_ref[...]
        half_gate = 0.5 * gate_ref[0, r, :]
        o_ref[0, r, :] = (o * (half_gate + half_gate * jnp.tanh(half_gate))).astype(o_ref.dtype)


def _hgrn(hq, hf, hi, hgate, lb_logits, hg_norm_g, seq):
    batch = hq.shape[0]
    rows = HG_STEP_CHUNKS * HG_CHUNK
    assert FRONT == rows, "the first grid step must hold exactly the rows in front of x"
    in_spec = pl.BlockSpec((1, rows, HG_WIDTH), lambda b, j: (b, j, 0))
    zmat = _hgrn_exponent_matrix()
    masks = _hgrn_level_masks()
    return pl.pallas_call(
        _hgrn_kernel,
        grid=(batch, (FRONT + seq) // rows),
        in_specs=[in_spec, in_spec, in_spec, in_spec,
                  _const_spec(lb_logits.shape), _const_spec((1, HG_WIDTH)),
                  _const_spec(zmat.shape), _const_spec(masks.shape)],
        out_specs=pl.BlockSpec((1, rows, HG_WIDTH), lambda b, j: (b, jnp.maximum(j - 1, 0), 0)),
        out_shape=jax.ShapeDtypeStruct((batch, seq, HG_WIDTH), BF16),
        scratch_shapes=[pltpu.VMEM((HG_HEADS, HG_DIM, HG_DIM), F32)],
        compiler_params=pltpu.CompilerParams(
            dimension_semantics=("parallel", "arbitrary"), vmem_limit_bytes=VMEM_LIMIT),
        name="hgrn2",
    )(hq, hf, hi, hgate, lb_logits, hg_norm_g, zmat, masks)


def _ffn_kernel(x_ref, osb_ref, ohg_ref, wout_ref, g2_ref, wg_ref, wu_ref, wd_ref, gf_ref, o_ref):
    mix = (jnp.dot(osb_ref[0], wout_ref[0:SB_WIDTH, :], preferred_element_type=F32)
           + jnp.dot(ohg_ref[0], wout_ref[SB_WIDTH:, :], preferred_element_type=F32))
    h1 = x_ref[0] + mix
    u2 = (h1 * lax.rsqrt(jnp.mean(h1 * h1, axis=-1, keepdims=True) + RMS_EPS)
          * g2_ref[...]).astype(BF16)
    ffn = jnp.zeros_like(h1)
    width = D_FF // FFN_SPLIT
    for n in range(FFN_SPLIT):
        cols = slice(n * width, (n + 1) * width)
        gate = jnp.dot(u2, wg_ref[:, cols], preferred_element_type=F32)
        up = jnp.dot(u2, wu_ref[:, cols], preferred_element_type=F32)
        half_gate = 0.5 * gate
        act = ((half_gate + half_gate * jnp.tanh(half_gate)) * up).astype(BF16)
        ffn = ffn + jnp.dot(act, wd_ref[cols, :], preferred_element_type=F32)
    h2 = h1 + ffn
    o_ref[0] = (h2 * lax.rsqrt(jnp.mean(h2 * h2, axis=-1, keepdims=True) + RMS_EPS)
                * gf_ref[...])


def _out_projection_ffn(x, o_sb, o_hg, w_out, norm2_g, w_gate, w_up, w_down, final_g):
    batch, seq, _ = x.shape
    row_spec = lambda width: pl.BlockSpec((1, FFN_ROWS, width), lambda b, j: (b, j, 0))
    resident = lambda shape: pl.BlockSpec(shape, lambda b, j: (0, 0), pipeline_mode=pl.Buffered(1))
    return pl.pallas_call(
        _ffn_kernel,
        grid=(batch, seq // FFN_ROWS),
        in_specs=[row_spec(D_MODEL), row_spec(SB_WIDTH), row_spec(HG_WIDTH),
                  resident((D_MODEL, D_MODEL)), resident((1, D_MODEL)),
                  resident((D_MODEL, D_FF)), resident((D_MODEL, D_FF)), resident((D_FF, D_MODEL)),
                  resident((1, D_MODEL))],
        out_specs=row_spec(D_MODEL),
        out_shape=jax.ShapeDtypeStruct((batch, seq, D_MODEL), x.dtype),
        compiler_params=pltpu.CompilerParams(
            dimension_semantics=("parallel", "parallel"), vmem_limit_bytes=VMEM_LIMIT),
        name="out_projection_ffn",
    )(x, o_sb, o_hg, w_out, norm2_g, w_gate, w_up, w_down, final_g)


def kernel(x, meta_tokens, norm1_g, w_in, sb_norm_g, hg_norm_g, hg_lb_logits, w_out, norm2_g,
           w_gate, w_up, w_down, final_norm_g):
    batch, seq, d_model = x.shape
    assert d_model == D_MODEL and seq % FFN_ROWS == 0 and seq % ATT_TILE == 0
    assert norm1_g.shape[0] == 1, "single-layer block"
    head = jnp.concatenate(
        [jnp.zeros((FIRST_VALID, D_MODEL), x.dtype), meta_tokens.astype(x.dtype)], axis=0)
    row = lambda g: g.reshape(1, -1).astype(F32)

    qkv, hq, hf, hi, hgate = _input_projection(x, head, row(norm1_g[0]), w_in[0].astype(F32))
    o_sb = _attention(qkv, row(sb_norm_g[0]), seq)
    o_hg = _hgrn(hq, hf, hi, hgate, hg_lb_logits.astype(F32), row(hg_norm_g[0]), seq)
    return _out_projection_ffn(
        x, o_sb, o_hg, w_out[0].astype(BF16), row(norm2_g[0]), w_gate[0].astype(BF16),
        w_up[0].astype(BF16), w_down[0].astype(BF16), row(final_norm_g))
```

```python
import functools

import jax
import jax.numpy as jnp
import numpy as np
from jax import lax
from jax.experimental import pallas as pl
from jax.experimental.pallas import tpu as pltpu

F32 = jnp.float32
BF16 = jnp.bfloat16

D_MODEL = 1024
N_META = 16
SB_HEADS = 8
SB_HEAD_DIM = 64
SB_WIDTH = SB_HEADS * SB_HEAD_DIM
HG_HEADS = 4
HG_DIM = 128
HG_WIDTH = HG_HEADS * HG_DIM
IN_WIDTH = 3 * SB_WIDTH + 4 * HG_WIDTH
D_FF = 2816
RMS_EPS = 1e-6

LANES = 128
SUBLANES = 8
FRONT = 256
FIRST_VALID = FRONT - N_META
ATT_TILE = 256
ATT_PAIRS = 4
HG_CHUNK = 128
HG_LEVELS = 7
HG_STEP_CHUNKS = 2
PROJ_ROWS = 256
FFN_ROWS = 512
FFN_SPLIT = 2
VMEM_LIMIT = 56 * 1024 * 1024
Q_SCALE = float(np.log2(np.e)) * SB_HEAD_DIM ** -0.5
MASKED = -1e9
SP_LINEAR = 64.0
EXIT_SUM = 160.0

NT_DIMS = (((1,), (1,)), ((), ()))
TN_DIMS = (((0,), (0,)), ((), ()))


def _const_spec(shape):
    return pl.BlockSpec(shape, lambda *_: (0,) * len(shape))


def _inproj_kernel(x_ref, head_ref, g_ref, w32_ref, qkv_ref, hq_ref, hf_ref, hi_ref, hg_ref, w_ref):
    j = pl.program_id(1)

    @pl.when(j == 0)
    def _():
        w_ref[...] = w32_ref[...].astype(BF16)

    h = jnp.where(j == 0, head_ref[...], x_ref[0])
    u = h * lax.rsqrt(jnp.mean(h * h, axis=-1, keepdims=True) + RMS_EPS) * g_ref[...]
    ub = u.astype(BF16)

    def proj(lo, width):
        return jnp.dot(ub, w_ref[:, lo:lo + width], preferred_element_type=F32)

    qkv_ref[0, :, 0:SB_WIDTH] = (proj(0, SB_WIDTH) * Q_SCALE).astype(BF16)
    qkv_ref[0, :, SB_WIDTH:3 * SB_WIDTH] = proj(SB_WIDTH, 2 * SB_WIDTH).astype(BF16)
    base = 3 * SB_WIDTH
    hq_ref[0] = proj(base, HG_WIDTH).astype(BF16)
    hf_ref[0] = proj(base + HG_WIDTH, HG_WIDTH)
    hi_ref[0] = proj(base + 2 * HG_WIDTH, HG_WIDTH).astype(BF16)
    hg_ref[0] = proj(base + 3 * HG_WIDTH, HG_WIDTH)


def _input_projection(x, head, norm_g, w_in):
    batch, seq, _ = x.shape
    n_tiles = (FRONT + seq) // PROJ_ROWS
    rows = FRONT + seq
    row_spec = lambda width: pl.BlockSpec((1, PROJ_ROWS, width), lambda b, j: (b, j, 0))
    return pl.pallas_call(
        _inproj_kernel,
        grid=(batch, n_tiles),
        in_specs=[
            pl.BlockSpec((1, PROJ_ROWS, D_MODEL), lambda b, j: (b, jnp.maximum(j - 1, 0), 0)),
            _const_spec((PROJ_ROWS, D_MODEL)),
            _const_spec((1, D_MODEL)),
            pl.BlockSpec((D_MODEL, IN_WIDTH), lambda b, j: (0, 0), pipeline_mode=pl.Buffered(1)),
        ],
        out_specs=[row_spec(3 * SB_WIDTH), row_spec(HG_WIDTH), row_spec(HG_WIDTH),
                   row_spec(HG_WIDTH), row_spec(HG_WIDTH)],
        out_shape=[
            jax.ShapeDtypeStruct((batch, rows, 3 * SB_WIDTH), BF16),
            jax.ShapeDtypeStruct((batch, rows, HG_WIDTH), BF16),
            jax.ShapeDtypeStruct((batch, rows, HG_WIDTH), F32),
            jax.ShapeDtypeStruct((batch, rows, HG_WIDTH), BF16),
            jax.ShapeDtypeStruct((batch, rows, HG_WIDTH), F32),
        ],
        scratch_shapes=[pltpu.VMEM((D_MODEL, IN_WIDTH), BF16)],
        compiler_params=pltpu.CompilerParams(
            dimension_semantics=("arbitrary", "arbitrary"), vmem_limit_bytes=VMEM_LIMIT),
        name="in_projection",
    )(x, head, norm_g, w_in)


def _attn_kernel(q_ref, k_ref, v_ref, tri_ref, g_ref, o_ref, qh_ref, z_ref, c_ref, sum_ref, acc_ref):
    tile = ATT_TILE
    n_heads = qh_ref.shape[0]
    i = pl.program_id(2) + FRONT // ATT_TILE
    last = i - FIRST_VALID // ATT_TILE
    lane = lax.broadcasted_iota(jnp.int32, (1, LANES), 1)
    first_head = lane < SB_HEAD_DIM
    t_pos = i * tile + lax.broadcasted_iota(jnp.int32, (1, tile), 1)

    def pair_lanes(h):
        return slice((h // 2) * LANES, (h // 2 + 1) * LANES)

    def rows_of(c):
        return pl.ds(pl.multiple_of(c * tile, tile), tile)

    def minus_row(a, row8):
        return (a.reshape(tile // SUBLANES, SUBLANES, tile) - row8[None]).reshape(tile, tile)

    def process(positions, fresh=False):
        visible = []
        for n in positions:
            s_pos = (i - n) * tile + lax.broadcasted_iota(jnp.int32, (tile, 1), 0)
            visible.append((s_pos < t_pos) & (s_pos >= FIRST_VALID))

        def scores(h):
            for slot, n in enumerate(positions):
                z = lax.dot_general(k_ref[0, rows_of(i - n), pair_lanes(h)], qh_ref[h], NT_DIMS,
                                    preferred_element_type=F32)
                z_ref[slot, h] = jnp.where(visible[slot], z, MASKED)

        def suffix_sums(h):
            for slot in range(len(positions)):
                zb = z_ref[slot, h].astype(BF16)
                grown = 1.0 + jnp.exp2(jnp.minimum(zb, SP_LINEAR))
                sp = jnp.maximum(zb, jnp.log2(grown.astype(F32)).astype(BF16))
                c_ref[slot, h] = jnp.dot(tri_ref[...], sp, preferred_element_type=F32)

        def weights(h):
            later = sum_ref[h]
            acc = acc_ref[h]
            for slot, n in enumerate(positions):
                local = c_ref[slot, h]
                z = z_ref[slot, h] if fresh and slot == 0 else minus_row(z_ref[slot, h], later)
                w = jnp.exp2((z - local).astype(BF16))
                acc = acc + lax.dot_general(v_ref[0, rows_of(i - n), pair_lanes(h)], w, TN_DIMS,
                                            preferred_element_type=F32)
                later = later + jnp.broadcast_to(local[0:1, :], (SUBLANES, tile))
            sum_ref[h] = later
            acc_ref[h] = acc

        phases = (scores, suffix_sums, weights)
        if len(positions) == 1:
            for phase in phases:
                for h in range(n_heads):
                    phase(h)
            return
        for k in range(n_heads + len(phases) - 1):
            for lag, phase in enumerate(phases):
                if 0 <= k - lag < n_heads:
                    phase(k - lag)

    for h in range(n_heads):
        q_pair = q_ref[0, :, pair_lanes(h)]
        qh_ref[h] = jnp.where(first_head == (h % 2 == 0), q_pair, jnp.zeros_like(q_pair))
    sum_ref[...] = jnp.zeros(sum_ref.shape, F32)
    acc_ref[...] = jnp.zeros(acc_ref.shape, F32)
    process([0, 1], fresh=True)

    def sweep(carry):
        n, _ = carry
        process([n])
        return n + 1, jnp.min(sum_ref[...])

    lax.while_loop(lambda carry: (carry[0] <= last) & (carry[1] < EXIT_SUM),
                   sweep, (jnp.int32(2), jnp.min(sum_ref[...])))

    for p in range(n_heads // 2):
        o = jnp.concatenate([acc_ref[2 * p, 0:SB_HEAD_DIM], acc_ref[2 * p + 1, SB_HEAD_DIM:LANES]],
                            axis=0).T
        sq = o * o
        ss_first = jnp.sum(jnp.where(first_head, sq, 0.0), axis=-1, keepdims=True)
        ss_second = jnp.sum(sq, axis=-1, keepdims=True) - ss_first
        mean_sq = jnp.where(first_head, ss_first, ss_second) * (1.0 / SB_HEAD_DIM)
        lanes = slice(p * LANES, (p + 1) * LANES)
        o_ref[0, :, lanes] = (o * lax.rsqrt(mean_sq + RMS_EPS) * g_ref[:, lanes]).astype(o_ref.dtype)


def _reverse_cumsum_matrix(n):
    j = np.arange(n)[:, None]
    s = np.arange(n)[None, :]
    return jnp.asarray((s >= j).astype(np.float32), BF16)


def _attention(qkv, sb_norm_g, seq):
    batch, rows, _ = qkv.shape
    width = ATT_PAIRS * LANES
    groups = SB_WIDTH // width
    n_q = seq // ATT_TILE
    heads = 2 * ATT_PAIRS
    return pl.pallas_call(
        _attn_kernel,
        grid=(batch, groups, n_q),
        in_specs=[
            pl.BlockSpec((1, ATT_TILE, width), lambda b, g, i: (b, i + FRONT // ATT_TILE, g)),
            pl.BlockSpec((1, rows, width), lambda b, g, i: (b, 0, groups + g)),
            pl.BlockSpec((1, rows, width), lambda b, g, i: (b, 0, 2 * groups + g)),
            _const_spec((ATT_TILE, ATT_TILE)),
            pl.BlockSpec((1, width), lambda b, g, i: (0, g)),
        ],
        out_specs=pl.BlockSpec((1, ATT_TILE, width), lambda b, g, i: (b, i, g)),
        out_shape=jax.ShapeDtypeStruct((batch, seq, SB_WIDTH), BF16),
        scratch_shapes=[pltpu.VMEM((heads, ATT_TILE, LANES), BF16),
                        pltpu.VMEM((2, heads, ATT_TILE, ATT_TILE), F32),
                        pltpu.VMEM((2, heads, ATT_TILE, ATT_TILE), F32),
                        pltpu.VMEM((heads, SUBLANES, ATT_TILE), F32),
                        pltpu.VMEM((heads, LANES, ATT_TILE), F32)],
        compiler_params=pltpu.CompilerParams(
            dimension_semantics=("parallel", "parallel", "parallel"), vmem_limit_bytes=VMEM_LIMIT),
        name="stickbreak_attention",
    )(qkv, qkv, qkv, _reverse_cumsum_matrix(ATT_TILE), sb_norm_g)


def _hgrn_exponent_matrix():
    c = HG_CHUNK
    t = np.arange(c)[:, None]
    j = np.arange(c)[None, :]
    mats = [(j <= t)]
    for lvl in range(HG_LEVELS):
        m = c >> (lvl + 1)
        blk = t // m
        odd = (blk % 2) == 1
        prefix = (j >= blk * m) & (j <= t)
        suffix = (j > t) & (j <= blk * m + m - 1)
        mats.append(np.where(odd, prefix, suffix))
    z = np.concatenate(mats, axis=0).astype(np.float32)
    return jnp.asarray(np.concatenate([z, z], axis=1), BF16)


def _hgrn_level_masks():
    c = HG_CHUNK
    t = np.arange(c)[:, None]
    s = np.arange(c)[None, :]
    masks = [t == s]
    for lvl in range(HG_LEVELS):
        m = c >> (lvl + 1)
        masks.append(((t // m) == (s // m) + 1) & (((s // m) % 2) == 0))
    return jnp.asarray(np.stack(masks).astype(np.float32))


def _hgrn_kernel(q_ref, f_ref, i_ref, gate_ref, lbl_ref, ng_ref, zmat_ref, mask_ref, o_ref,
                 state_ref):
    j = pl.program_id(1)
    ch = HG_CHUNK
    rows = HG_STEP_CHUNKS * ch

    @pl.when(j == 0)
    def _():
        state_ref[...] = jnp.zeros_like(state_ref)

    logits = lbl_ref[...]
    mx = jnp.max(logits, axis=0, keepdims=True)
    ex = jnp.exp(logits - mx)
    lb = ex[0:1] / jnp.sum(ex, axis=0, keepdims=True)

    f_raw = f_ref[0]
    sig = 0.5 + 0.5 * jnp.tanh(0.5 * f_raw)
    log_f = jnp.log2(lb + (1.0 - lb) * sig)
    kk = (1.0 - lb) * (1.0 - sig)
    row_all = lax.broadcasted_iota(jnp.int32, (rows, 1), 0)
    valid = (j > 0) | (row_all >= FIRST_VALID)
    log_f = jnp.where(valid, log_f, 0.0)
    kk = jnp.where(valid, kk, 0.0)
    hi = log_f.astype(BF16)
    lo = (log_f - hi.astype(F32)).astype(BF16)
    q_all = q_ref[0].astype(F32)
    v_all = i_ref[0]

    chunks = [slice(n * ch, (n + 1) * ch) for n in range(HG_STEP_CHUNKS)]
    heads = [slice(h * HG_DIM, (h + 1) * HG_DIM) for h in range(HG_HEADS)]
    row = lax.broadcasted_iota(jnp.int32, (ch, 1), 0)
    expo, sides, q_bf, k_bf, q_inter, k_state, state_decay = {}, {}, {}, {}, {}, {}, {}
    parts, scores, intra, kv = {}, {}, {}, {}

    def exponents(n):
        r = chunks[n]
        expo[n] = jnp.dot(zmat_ref[...], jnp.concatenate([hi[r], lo[r]], axis=0),
                          preferred_element_type=F32)

    def decayed_operands(n):
        q, k = q_all[chunks[n]], kk[chunks[n]]
        level_sides = []
        for lvl in range(HG_LEVELS):
            m = ch >> (lvl + 1)
            decay = jnp.exp2(expo[n][(lvl + 1) * ch:(lvl + 2) * ch])
            if m % SUBLANES == 0:
                both = jnp.concatenate([(q if blk % 2 else k)[blk * m:(blk + 1) * m]
                                        for blk in range(ch // m)], axis=0)
            else:
                both = jnp.where(((row // m) % 2) == 1, q, k)
            level_sides.append((both * decay).astype(BF16))
        sides[n] = level_sides
        q_bf[n] = q.astype(BF16)
        k_bf[n] = k.astype(BF16)
        b = expo[n][0:ch]
        b_last = b[ch - 1:ch]
        q_inter[n] = (q * jnp.exp2(b)).astype(BF16)
        k_state[n] = (k * jnp.exp2(b_last - b)).astype(BF16)
        state_decay[n] = jnp.exp2(b_last)

    def level_products(n):
        parts[n] = [[lax.dot_general(x[:, sl], y[:, sl], NT_DIMS, preferred_element_type=F32)
                     for x, y in [(q_bf[n], k_bf[n])] + [(sd, sd) for sd in sides[n]]]
                    for sl in heads]

    def assemble_scores(n):
        per_head = []
        for h in range(HG_HEADS):
            sc = parts[n][h][0] * mask_ref[0]
            for lvl in range(HG_LEVELS):
                sc = sc + parts[n][h][lvl + 1] * mask_ref[lvl + 1]
            per_head.append(sc.astype(BF16))
        scores[n] = per_head

    def value_products(n):
        r = chunks[n]
        intra[n] = [jnp.dot(scores[n][h], v_all[r, sl], preferred_element_type=F32)
                    for h, sl in enumerate(heads)]
        kv[n] = [lax.dot_general(v_all[r, sl], k_state[n][:, sl], TN_DIMS,
                                 preferred_element_type=F32) for sl in heads]

    phases = (exponents, decayed_operands, level_products, assemble_scores, value_products)
    for k in range(HG_STEP_CHUNKS + len(phases) - 1):
        for lag, phase in enumerate(phases):
            if 0 <= k - lag < HG_STEP_CHUNKS:
                phase(k - lag)

    for n, r in enumerate(chunks):
        outs = []
        for h, sl in enumerate(heads):
            st = state_ref[h]
            o_h = intra[n][h] + lax.dot_general(q_inter[n][:, sl], st.astype(BF16), NT_DIMS,
                                                preferred_element_type=F32)
            state_ref[h] = st * state_decay[n][:, sl] + kv[n][h]
            outs.append(o_h * lax.rsqrt(jnp.mean(o_h * o_h, axis=-1, keepdims=True) + RMS_EPS))
        o = jnp.concatenate(outs, axis=1) * ng_ref[...]
        half_gate = 0.5 * gate_ref[0, r, :]
        o_ref[0, r, :] = (o * (half_gate + half_gate * jnp.tanh(half_gate))).astype(o_ref.dtype)


def _hgrn(hq, hf, hi, hgate, lb_logits, hg_norm_g, seq):
    batch = hq.shape[0]
    rows = HG_STEP_CHUNKS * HG_CHUNK
    assert FRONT == rows, "the first grid step must hold exactly the rows in front of x"
    in_spec = pl.BlockSpec((1, rows, HG_WIDTH), lambda b, j: (b, j, 0))
    zmat = _hgrn_exponent_matrix()
    masks = _hgrn_level_masks()
    return pl.pallas_call(
        _hgrn_kernel,
        grid=(batch, (FRONT + seq) // rows),
        in_specs=[in_spec, in_spec, in_spec, in_spec,
                  _const_spec(lb_logits.shape), _const_spec((1, HG_WIDTH)),
                  _const_spec(zmat.shape), _const_spec(masks.shape)],
        out_specs=pl.BlockSpec((1, rows, HG_WIDTH), lambda b, j: (b, jnp.maximum(j - 1, 0), 0)),
        out_shape=jax.ShapeDtypeStruct((batch, seq, HG_WIDTH), BF16),
        scratch_shapes=[pltpu.VMEM((HG_HEADS, HG_DIM, HG_DIM), F32)],
        compiler_params=pltpu.CompilerParams(
            dimension_semantics=("parallel", "arbitrary"), vmem_limit_bytes=VMEM_LIMIT),
        name="hgrn2",
    )(hq, hf, hi, hgate, lb_logits, hg_norm_g, zmat, masks)


def _ffn_kernel(x_ref, osb_ref, ohg_ref, wout_ref, g2_ref, wg_ref, wu_ref, wd_ref, gf_ref, o_ref):
    half = FFN_ROWS // 2
    width = D_FF // FFN_SPLIT
    h1, u2, ffn = {}, {}, {}

    def mix_and_norm(r):
        rows = slice(r * half, (r + 1) * half)
        mix = (jnp.dot(osb_ref[0, rows, :], wout_ref[0:SB_WIDTH, :], preferred_element_type=F32)
               + jnp.dot(ohg_ref[0, rows, :], wout_ref[SB_WIDTH:, :], preferred_element_type=F32))
        h1[r] = x_ref[0, rows, :] + mix
        u2[r] = (h1[r] * lax.rsqrt(jnp.mean(h1[r] * h1[r], axis=-1, keepdims=True) + RMS_EPS)
                 * g2_ref[...]).astype(BF16)

    def swiglu(r):
        acc = jnp.zeros_like(h1[r])
        for n in range(FFN_SPLIT):
            cols = slice(n * width, (n + 1) * width)
            gate = jnp.dot(u2[r], wg_ref[:, cols], preferred_element_type=F32)
            up = jnp.dot(u2[r], wu_ref[:, cols], preferred_element_type=F32)
            half_gate = 0.5 * gate
            act = ((half_gate + half_gate * jnp.tanh(half_gate)) * up).astype(BF16)
            acc = acc + jnp.dot(act, wd_ref[cols, :], preferred_element_type=F32)
        ffn[r] = acc

    def final_norm(r):
        rows = slice(r * half, (r + 1) * half)
        h2 = h1[r] + ffn[r]
        o_ref[0, rows, :] = (h2 * lax.rsqrt(jnp.mean(h2 * h2, axis=-1, keepdims=True) + RMS_EPS)
                             * gf_ref[...])

    phases = (mix_and_norm, swiglu, final_norm)
    for k in range(2 + len(phases) - 1):
        for lag, phase in enumerate(phases):
            if 0 <= k - lag < 2:
                phase(k - lag)


def _out_projection_ffn(x, o_sb, o_hg, w_out, norm2_g, w_gate, w_up, w_down, final_g):
    batch, seq, _ = x.shape
    row_spec = lambda width: pl.BlockSpec((1, FFN_ROWS, width), lambda b, j: (b, j, 0))
    resident = lambda shape: pl.BlockSpec(shape, lambda b, j: (0, 0), pipeline_mode=pl.Buffered(1))
    return pl.pallas_call(
        _ffn_kernel,
        grid=(batch, seq // FFN_ROWS),
        in_specs=[row_spec(D_MODEL), row_spec(SB_WIDTH), row_spec(HG_WIDTH),
                  resident((D_MODEL, D_MODEL)), resident((1, D_MODEL)),
                  resident((D_MODEL, D_FF)), resident((D_MODEL, D_FF)), resident((D_FF, D_MODEL)),
                  resident((1, D_MODEL))],
        out_specs=row_spec(D_MODEL),
        out_shape=jax.ShapeDtypeStruct((batch, seq, D_MODEL), x.dtype),
        compiler_params=pltpu.CompilerParams(
            dimension_semantics=("parallel", "parallel"), vmem_limit_bytes=VMEM_LIMIT),
        name="out_projection_ffn",
    )(x, o_sb, o_hg, w_out, norm2_g, w_gate, w_up, w_down, final_g)


def kernel(x, meta_tokens, norm1_g, w_in, sb_norm_g, hg_norm_g, hg_lb_logits, w_out, norm2_g,
           w_gate, w_up, w_down, final_norm_g):
    batch, seq, d_model = x.shape
    assert d_model == D_MODEL and seq % FFN_ROWS == 0 and seq % ATT_TILE == 0
    assert norm1_g.shape[0] == 1, "single-layer block"
    head = jnp.concatenate(
        [jnp.zeros((FIRST_VALID, D_MODEL), x.dtype), meta_tokens.astype(x.dtype)], axis=0)
    row = lambda g: g.reshape(1, -1).astype(F32)

    qkv, hq, hf, hi, hgate = _input_projection(x, head, row(norm1_g[0]), w_in[0].astype(F32))
    o_sb = _attention(qkv, row(sb_norm_g[0]), seq)
    o_hg = _hgrn(hq, hf, hi, hgate, hg_lb_logits.astype(F32), row(hg_norm_g[0]), seq)
    return _out_projection_ffn(
        x, o_sb, o_hg, w_out[0].astype(BF16), row(norm2_g[0]), w_gate[0].astype(BF16),
        w_up[0].astype(BF16), w_down[0].astype(BF16), row(final_norm_g))
```

```python
import functools

import jax
import jax.numpy as jnp
import numpy as np
from jax import lax
from jax.experimental import pallas as pl
from jax.experimental.pallas import tpu as pltpu

F32 = jnp.float32
BF16 = jnp.bfloat16

D_MODEL = 1024
N_META = 16
SB_HEADS = 8
SB_HEAD_DIM = 64
SB_WIDTH = SB_HEADS * SB_HEAD_DIM
HG_HEADS = 4
HG_DIM = 128
HG_WIDTH = HG_HEADS * HG_DIM
IN_WIDTH = 3 * SB_WIDTH + 4 * HG_WIDTH
D_FF = 2816
RMS_EPS = 1e-6

LANES = 128
SUBLANES = 8
FRONT = 256
FIRST_VALID = FRONT - N_META
ATT_TILE = 256
ATT_PAIRS = 4
HG_CHUNK = 128
HG_LEVELS = 7
HG_STEP_CHUNKS = 2
PROJ_ROWS = 256
FFN_ROWS = 512
FFN_SPLIT = 2
VMEM_LIMIT = 56 * 1024 * 1024
Q_SCALE = float(np.log2(np.e)) * SB_HEAD_DIM ** -0.5
MASKED = -1e9
SP_LINEAR = 64.0
EXIT_SUM = 160.0

NT_DIMS = (((1,), (1,)), ((), ()))
TN_DIMS = (((0,), (0,)), ((), ()))


def _const_spec(shape):
    return pl.BlockSpec(shape, lambda *_: (0,) * len(shape))


def _inproj_kernel(x_ref, head_ref, g_ref, w32_ref, qkv_ref, hq_ref, hf_ref, hi_ref, hg_ref, w_ref):
    j = pl.program_id(1)

    @pl.when(j == 0)
    def _():
        w_ref[...] = w32_ref[...].astype(BF16)

    h = jnp.where(j == 0, head_ref[...], x_ref[0])
    u = h * lax.rsqrt(jnp.mean(h * h, axis=-1, keepdims=True) + RMS_EPS) * g_ref[...]
    ub = u.astype(BF16)

    def proj(lo, width):
        return jnp.dot(ub, w_ref[:, lo:lo + width], preferred_element_type=F32)

    qkv_ref[0, :, 0:SB_WIDTH] = (proj(0, SB_WIDTH) * Q_SCALE).astype(BF16)
    qkv_ref[0, :, SB_WIDTH:3 * SB_WIDTH] = proj(SB_WIDTH, 2 * SB_WIDTH).astype(BF16)
    base = 3 * SB_WIDTH
    hq_ref[0] = proj(base, HG_WIDTH).astype(BF16)
    hf_ref[0] = proj(base + HG_WIDTH, HG_WIDTH)
    hi_ref[0] = proj(base + 2 * HG_WIDTH, HG_WIDTH).astype(BF16)
    hg_ref[0] = proj(base + 3 * HG_WIDTH, HG_WIDTH)


def _input_projection(x, head, norm_g, w_in):
    batch, seq, _ = x.shape
    n_tiles = (FRONT + seq) // PROJ_ROWS
    rows = FRONT + seq
    row_spec = lambda width: pl.BlockSpec((1, PROJ_ROWS, width), lambda b, j: (b, j, 0))
    return pl.pallas_call(
        _inproj_kernel,
        grid=(batch, n_tiles),
        in_specs=[
            pl.BlockSpec((1, PROJ_ROWS, D_MODEL), lambda b, j: (b, jnp.maximum(j - 1, 0), 0)),
            _const_spec((PROJ_ROWS, D_MODEL)),
            _const_spec((1, D_MODEL)),
            pl.BlockSpec((D_MODEL, IN_WIDTH), lambda b, j: (0, 0), pipeline_mode=pl.Buffered(1)),
        ],
        out_specs=[row_spec(3 * SB_WIDTH), row_spec(HG_WIDTH), row_spec(HG_WIDTH),
                   row_spec(HG_WIDTH), row_spec(HG_WIDTH)],
        out_shape=[
            jax.ShapeDtypeStruct((batch, rows, 3 * SB_WIDTH), BF16),
            jax.ShapeDtypeStruct((batch, rows, HG_WIDTH), BF16),
            jax.ShapeDtypeStruct((batch, rows, HG_WIDTH), F32),
            jax.ShapeDtypeStruct((batch, rows, HG_WIDTH), BF16),
            jax.ShapeDtypeStruct((batch, rows, HG_WIDTH), F32),
        ],
        scratch_shapes=[pltpu.VMEM((D_MODEL, IN_WIDTH), BF16)],
        compiler_params=pltpu.CompilerParams(
            dimension_semantics=("arbitrary", "arbitrary"), vmem_limit_bytes=VMEM_LIMIT),
        name="in_projection",
    )(x, head, norm_g, w_in)


def _attn_kernel(q_ref, k_ref, v_ref, tri_ref, g_ref, o_ref, qh_ref, z_ref, c_ref, sum_ref, acc_ref):
    tile = ATT_TILE
    n_heads = qh_ref.shape[0]
    i = pl.program_id(2) + FRONT // ATT_TILE
    last = i - FIRST_VALID // ATT_TILE
    lane = lax.broadcasted_iota(jnp.int32, (1, LANES), 1)
    first_head = lane < SB_HEAD_DIM
    t_pos = i * tile + lax.broadcasted_iota(jnp.int32, (1, tile), 1)

    def pair_lanes(h):
        return slice((h // 2) * LANES, (h // 2 + 1) * LANES)

    def rows_of(c):
        return pl.ds(pl.multiple_of(c * tile, tile), tile)

    def minus_row(a, row8):
        return (a.reshape(tile // SUBLANES, SUBLANES, tile) - row8[None]).reshape(tile, tile)

    def process(positions, from_diagonal=False):
        half = tile // 2

        def on_visible_quadrants(fn, *arrays):
            top = fn(*[a[0:half, :] for a in arrays])
            corner = fn(*[a[half:, half:] for a in arrays])
            return jnp.concatenate(
                [top, jnp.concatenate([jnp.zeros_like(corner), corner], axis=1)], axis=0)

        def softplus_bf16(z):
            zb = z.astype(BF16)
            grown = 1.0 + jnp.exp2(jnp.minimum(zb, SP_LINEAR))
            return jnp.maximum(zb, jnp.log2(grown.astype(F32)).astype(BF16))

        def weight_bf16(y, local):
            return jnp.exp2((y - local).astype(BF16))
        visible = []
        for n in positions:
            s_pos = (i - n) * tile + lax.broadcasted_iota(jnp.int32, (tile, 1), 0)
            visible.append((s_pos < t_pos) & (s_pos >= FIRST_VALID))

        def scores(h):
            for slot, n in enumerate(positions):
                z = lax.dot_general(k_ref[0, rows_of(i - n), pair_lanes(h)], qh_ref[h], NT_DIMS,
                                    preferred_element_type=F32)
                z_ref[slot, h] = jnp.where(visible[slot], z, MASKED)

        def suffix_sums(h):
            for slot in range(len(positions)):
                if from_diagonal and slot == 0:
                    sp = on_visible_quadrants(softplus_bf16, z_ref[slot, h])
                else:
                    sp = softplus_bf16(z_ref[slot, h])
                c_ref[slot, h] = jnp.dot(tri_ref[...], sp, preferred_element_type=F32)

        def weights(h):
            later = sum_ref[h]
            acc = acc_ref[h]
            for slot, n in enumerate(positions):
                local = c_ref[slot, h]
                if from_diagonal and slot == 0:
                    w = on_visible_quadrants(weight_bf16, z_ref[slot, h], local)
                else:
                    w = weight_bf16(minus_row(z_ref[slot, h], later), local)
                acc = acc + lax.dot_general(v_ref[0, rows_of(i - n), pair_lanes(h)], w, TN_DIMS,
                                            preferred_element_type=F32)
                later = later + jnp.broadcast_to(local[0:1, :], (SUBLANES, tile))
            sum_ref[h] = later
            acc_ref[h] = acc

        phases = (scores, suffix_sums, weights)
        if len(positions) == 1:
            for phase in phases:
                for h in range(n_heads):
                    phase(h)
            return
        for k in range(n_heads + len(phases) - 1):
            for lag, phase in enumerate(phases):
                if 0 <= k - lag < n_heads:
                    phase(k - lag)

    for h in range(n_heads):
        q_pair = q_ref[0, :, pair_lanes(h)]
        qh_ref[h] = jnp.where(first_head == (h % 2 == 0), q_pair, jnp.zeros_like(q_pair))
    sum_ref[...] = jnp.zeros(sum_ref.shape, F32)
    acc_ref[...] = jnp.zeros(acc_ref.shape, F32)
    process([0, 1], from_diagonal=True)

    def sweep(carry):
        n, _ = carry
        process([n])
        return n + 1, jnp.min(sum_ref[...])

    lax.while_loop(lambda carry: (carry[0] <= last) & (carry[1] < EXIT_SUM),
                   sweep, (jnp.int32(2), jnp.min(sum_ref[...])))

    for p in range(n_heads // 2):
        o = jnp.concatenate([acc_ref[2 * p, 0:SB_HEAD_DIM], acc_ref[2 * p + 1, SB_HEAD_DIM:LANES]],
                            axis=0).T
        sq = o * o
        ss_first = jnp.sum(jnp.where(first_head, sq, 0.0), axis=-1, keepdims=True)
        ss_second = jnp.sum(sq, axis=-1, keepdims=True) - ss_first
        mean_sq = jnp.where(first_head, ss_first, ss_second) * (1.0 / SB_HEAD_DIM)
        lanes = slice(p * LANES, (p + 1) * LANES)
        o_ref[0, :, lanes] = (o * lax.rsqrt(mean_sq + RMS_EPS) * g_ref[:, lanes]).astype(o_ref.dtype)


def _reverse_cumsum_matrix(n):
    j = np.arange(n)[:, None]
    s = np.arange(n)[None, :]
    return jnp.asarray((s >= j).astype(np.float32), BF16)


def _attention(qkv, sb_norm_g, seq):
    batch, rows, _ = qkv.shape
    width = ATT_PAIRS * LANES
    groups = SB_WIDTH // width
    n_q = seq // ATT_TILE
    heads = 2 * ATT_PAIRS
    return pl.pallas_call(
        _attn_kernel,
        grid=(batch, groups, n_q),
        in_specs=[
            pl.BlockSpec((1, ATT_TILE, width), lambda b, g, i: (b, i + FRONT // ATT_TILE, g)),
            pl.BlockSpec((1, rows, width), lambda b, g, i: (b, 0, groups + g)),
            pl.BlockSpec((1, rows, width), lambda b, g, i: (b, 0, 2 * groups + g)),
            _const_spec((ATT_TILE, ATT_TILE)),
            pl.BlockSpec((1, width), lambda b, g, i: (0, g)),
        ],
        out_specs=pl.BlockSpec((1, ATT_TILE, width), lambda b, g, i: (b, i, g)),
        out_shape=jax.ShapeDtypeStruct((batch, seq, SB_WIDTH), BF16),
        scratch_shapes=[pltpu.VMEM((heads, ATT_TILE, LANES), BF16),
                        pltpu.VMEM((2, heads, ATT_TILE, ATT_TILE), F32),
                        pltpu.VMEM((2, heads, ATT_TILE, ATT_TILE), F32),
                        pltpu.VMEM((heads, SUBLANES, ATT_TILE), F32),
                        pltpu.VMEM((heads, LANES, ATT_TILE), F32)],
        compiler_params=pltpu.CompilerParams(
            dimension_semantics=("parallel", "parallel", "parallel"), vmem_limit_bytes=VMEM_LIMIT),
        name="stickbreak_attention",
    )(qkv, qkv, qkv, _reverse_cumsum_matrix(ATT_TILE), sb_norm_g)


def _hgrn_exponent_matrix():
    c = HG_CHUNK
    t = np.arange(c)[:, None]
    j = np.arange(c)[None, :]
    mats = [(j <= t)]
    for lvl in range(HG_LEVELS):
        m = c >> (lvl + 1)
        blk = t // m
        odd = (blk % 2) == 1
        prefix = (j >= blk * m) & (j <= t)
        suffix = (j > t) & (j <= blk * m + m - 1)
        mats.append(np.where(odd, prefix, suffix))
    z = np.concatenate(mats, axis=0).astype(np.float32)
    return jnp.asarray(np.concatenate([z, z], axis=1), BF16)


def _hgrn_level_masks():
    c = HG_CHUNK
    t = np.arange(c)[:, None]
    s = np.arange(c)[None, :]
    masks = [t == s]
    for lvl in range(HG_LEVELS):
        m = c >> (lvl + 1)
        masks.append(((t // m) == (s // m) + 1) & (((s // m) % 2) == 0))
    return jnp.asarray(np.stack(masks).astype(np.float32))


def _hgrn_kernel(q_ref, f_ref, i_ref, gate_ref, lbl_ref, ng_ref, zmat_ref, mask_ref, o_ref,
                 state_ref):
    j = pl.program_id(1)
    ch = HG_CHUNK
    rows = HG_STEP_CHUNKS * ch

    @pl.when(j == 0)
    def _():
        state_ref[...] = jnp.zeros_like(state_ref)

    logits = lbl_ref[...]
    mx = jnp.max(logits, axis=0, keepdims=True)
    ex = jnp.exp(logits - mx)
    lb = ex[0:1] / jnp.sum(ex, axis=0, keepdims=True)

    f_raw = f_ref[0]
    sig = 0.5 + 0.5 * jnp.tanh(0.5 * f_raw)
    log_f = jnp.log2(lb + (1.0 - lb) * sig)
    kk = (1.0 - lb) * (1.0 - sig)
    row_all = lax.broadcasted_iota(jnp.int32, (rows, 1), 0)
    valid = (j > 0) | (row_all >= FIRST_VALID)
    log_f = jnp.where(valid, log_f, 0.0)
    kk = jnp.where(valid, kk, 0.0)
    hi = log_f.astype(BF16)
    lo = (log_f - hi.astype(F32)).astype(BF16)
    q_all = q_ref[0].astype(F32)
    v_all = i_ref[0]

    chunks = [slice(n * ch, (n + 1) * ch) for n in range(HG_STEP_CHUNKS)]
    heads = [slice(h * HG_DIM, (h + 1) * HG_DIM) for h in range(HG_HEADS)]
    row = lax.broadcasted_iota(jnp.int32, (ch, 1), 0)
    expo, sides, q_bf, k_bf, q_inter, k_state, state_decay = {}, {}, {}, {}, {}, {}, {}
    parts, scores, intra, kv = {}, {}, {}, {}

    def exponents(n):
        r = chunks[n]
        expo[n] = jnp.dot(zmat_ref[...], jnp.concatenate([hi[r], lo[r]], axis=0),
                          preferred_element_type=F32)

    def decayed_operands(n):
        q, k = q_all[chunks[n]], kk[chunks[n]]
        level_sides = []
        for lvl in range(HG_LEVELS):
            m = ch >> (lvl + 1)
            decay = jnp.exp2(expo[n][(lvl + 1) * ch:(lvl + 2) * ch])
            if m % SUBLANES == 0:
                both = jnp.concatenate([(q if blk % 2 else k)[blk * m:(blk + 1) * m]
                                        for blk in range(ch // m)], axis=0)
            else:
                both = jnp.where(((row // m) % 2) == 1, q, k)
            level_sides.append((both * decay).astype(BF16))
        sides[n] = level_sides
        q_bf[n] = q.astype(BF16)
        k_bf[n] = k.astype(BF16)
        b = expo[n][0:ch]
        b_last = b[ch - 1:ch]
        q_inter[n] = (q * jnp.exp2(b)).astype(BF16)
        k_state[n] = (k * jnp.exp2(b_last - b)).astype(BF16)
        state_decay[n] = jnp.exp2(b_last)

    def level_products(n):
        parts[n] = [[lax.dot_general(x[:, sl], y[:, sl], NT_DIMS, preferred_element_type=F32)
                     for x, y in [(q_bf[n], k_bf[n])] + [(sd, sd) for sd in sides[n]]]
                    for sl in heads]

    def assemble_scores(n):
        per_head = []
        for h in range(HG_HEADS):
            sc = parts[n][h][0] * mask_ref[0]
            for lvl in range(HG_LEVELS):
                sc = sc + parts[n][h][lvl + 1] * mask_ref[lvl + 1]
            per_head.append(sc.astype(BF16))
        scores[n] = per_head

    def value_products(n):
        r = chunks[n]
        intra[n] = [jnp.dot(scores[n][h], v_all[r, sl], preferred_element_type=F32)
                    for h, sl in enumerate(heads)]
        kv[n] = [lax.dot_general(v_all[r, sl], k_state[n][:, sl], TN_DIMS,
                                 preferred_element_type=F32) for sl in heads]

    phases = (exponents, decayed_operands, level_products, assemble_scores, value_products)
    for k in range(HG_STEP_CHUNKS + len(phases) - 1):
        for lag, phase in enumerate(phases):
            if 0 <= k - lag < HG_STEP_CHUNKS:
                phase(k - lag)

    for n, r in enumerate(chunks):
        outs = []
        for h, sl in enumerate(heads):
            st = state_ref[h]
            o_h = intra[n][h] + lax.dot_general(q_inter[n][:, sl], st.astype(BF16), NT_DIMS,
                                                preferred_element_type=F32)
            state_ref[h] = st * state_decay[n][:, sl] + kv[n][h]
            outs.append(o_h * lax.rsqrt(jnp.mean(o_h * o_h, axis=-1, keepdims=True) + RMS_EPS))
        o = jnp.concatenate(outs, axis=1) * ng_ref[...]
        half_gate = 0.5 * gate_ref[0, r, :]
        o_ref[0, r, :] = (o * (half_gate + half_gate * jnp.tanh(half_gate))).astype(o_ref.dtype)


def _hgrn(hq, hf, hi, hgate, lb_logits, hg_norm_g, seq):
    batch = hq.shape[0]
    rows = HG_STEP_CHUNKS * HG_CHUNK
    assert FRONT == rows, "the first grid step must hold exactly the rows in front of x"
    in_spec = pl.BlockSpec((1, rows, HG_WIDTH), lambda b, j: (b, j, 0))
    zmat = _hgrn_exponent_matrix()
    masks = _hgrn_level_masks()
    return pl.pallas_call(
        _hgrn_kernel,
        grid=(batch, (FRONT + seq) // rows),
        in_specs=[in_spec, in_spec, in_spec, in_spec,
                  _const_spec(lb_logits.shape), _const_spec((1, HG_WIDTH)),
                  _const_spec(zmat.shape), _const_spec(masks.shape)],
        out_specs=pl.BlockSpec((1, rows, HG_WIDTH), lambda b, j: (b, jnp.maximum(j - 1, 0), 0)),
        out_shape=jax.ShapeDtypeStruct((batch, seq, HG_WIDTH), BF16),
        scratch_shapes=[pltpu.VMEM((HG_HEADS, HG_DIM, HG_DIM), F32)],
        compiler_params=pltpu.CompilerParams(
            dimension_semantics=("parallel", "arbitrary"), vmem_limit_bytes=VMEM_LIMIT),
        name="hgrn2",
    )(hq, hf, hi, hgate, lb_logits, hg_norm_g, zmat, masks)


def _ffn_kernel(x_ref, osb_ref, ohg_ref, wout_ref, g2_ref, wg_ref, wu_ref, wd_ref, gf_ref, o_ref):
    half = FFN_ROWS // 2
    width = D_FF // FFN_SPLIT
    h1, u2, ffn = {}, {}, {}

    def mix_and_norm(r):
        rows = slice(r * half, (r + 1) * half)
        mix = (jnp.dot(osb_ref[0, rows, :], wout_ref[0:SB_WIDTH, :], preferred_element_type=F32)
               + jnp.dot(ohg_ref[0, rows, :], wout_ref[SB_WIDTH:, :], preferred_element_type=F32))
        h1[r] = x_ref[0, rows, :] + mix
        u2[r] = (h1[r] * lax.rsqrt(jnp.mean(h1[r] * h1[r], axis=-1, keepdims=True) + RMS_EPS)
                 * g2_ref[...]).astype(BF16)

    def swiglu(r):
        acc = jnp.zeros_like(h1[r])
        for n in range(FFN_SPLIT):
            cols = slice(n * width, (n + 1) * width)
            gate = jnp.dot(u2[r], wg_ref[:, cols], preferred_element_type=F32)
            up = jnp.dot(u2[r], wu_ref[:, cols], preferred_element_type=F32)
            half_gate = 0.5 * gate
            act = ((half_gate + half_gate * jnp.tanh(half_gate)) * up).astype(BF16)
            acc = acc + jnp.dot(act, wd_ref[cols, :], preferred_element_type=F32)
        ffn[r] = acc

    def final_norm(r):
        rows = slice(r * half, (r + 1) * half)
        h2 = h1[r] + ffn[r]
        o_ref[0, rows, :] = (h2 * lax.rsqrt(jnp.mean(h2 * h2, axis=-1, keepdims=True) + RMS_EPS)
                             * gf_ref[...])

    phases = (mix_and_norm, swiglu, final_norm)
    for k in range(2 + len(phases) - 1):
        for lag, phase in enumerate(phases):
            if 0 <= k - lag < 2:
                phase(k - lag)


def _out_projection_ffn(x, o_sb, o_hg, w_out, norm2_g, w_gate, w_up, w_down, final_g):
    batch, seq, _ = x.shape
    row_spec = lambda width: pl.BlockSpec((1, FFN_ROWS, width), lambda b, j: (b, j, 0))
    resident = lambda shape: pl.BlockSpec(shape, lambda b, j: (0, 0), pipeline_mode=pl.Buffered(1))
    return pl.pallas_call(
        _ffn_kernel,
        grid=(batch, seq // FFN_ROWS),
        in_specs=[row_spec(D_MODEL), row_spec(SB_WIDTH), row_spec(HG_WIDTH),
                  resident((D_MODEL, D_MODEL)), resident((1, D_MODEL)),
                  resident((D_MODEL, D_FF)), resident((D_MODEL, D_FF)), resident((D_FF, D_MODEL)),
                  resident((1, D_MODEL))],
        out_specs=row_spec(D_MODEL),
        out_shape=jax.ShapeDtypeStruct((batch, seq, D_MODEL), x.dtype),
        compiler_params=pltpu.CompilerParams(
            dimension_semantics=("parallel", "parallel"), vmem_limit_bytes=VMEM_LIMIT),
        name="out_projection_ffn",
    )(x, o_sb, o_hg, w_out, norm2_g, w_gate, w_up, w_down, final_g)


def kernel(x, meta_tokens, norm1_g, w_in, sb_norm_g, hg_norm_g, hg_lb_logits, w_out, norm2_g,
           w_gate, w_up, w_down, final_norm_g):
    batch, seq, d_model = x.shape
    assert d_model == D_MODEL and seq % FFN_ROWS == 0 and seq % ATT_TILE == 0
    assert norm1_g.shape[0] == 1, "single-layer block"
    head = jnp.concatenate(
        [jnp.zeros((FIRST_VALID, D_MODEL), x.dtype), meta_tokens.astype(x.dtype)], axis=0)
    row = lambda g: g.reshape(1, -1).astype(F32)

    qkv, hq, hf, hi, hgate = _input_projection(x, head, row(norm1_g[0]), w_in[0].astype(F32))
    o_sb = _attention(qkv, row(sb_norm_g[0]), seq)
    o_hg = _hgrn(hq, hf, hi, hgate, hg_lb_logits.astype(F32), row(hg_norm_g[0]), seq)
    return _out_projection_ffn(
        x, o_sb, o_hg, w_out[0].astype(BF16), row(norm2_g[0]), w_gate[0].astype(BF16),
        w_up[0].astype(BF16), w_down[0].astype(BF16), row(final_norm_g))
```

```python
import functools

import jax
import jax.numpy as jnp
import numpy as np
from jax import lax
from jax.experimental import pallas as pl
from jax.experimental.pallas import tpu as pltpu

F32 = jnp.float32
BF16 = jnp.bfloat16

D_MODEL = 1024
N_META = 16
SB_HEADS = 8
SB_HEAD_DIM = 64
SB_WIDTH = SB_HEADS * SB_HEAD_DIM
HG_HEADS = 4
HG_DIM = 128
HG_WIDTH = HG_HEADS * HG_DIM
IN_WIDTH = 3 * SB_WIDTH + 4 * HG_WIDTH
D_FF = 2816
RMS_EPS = 1e-6

LANES = 128
SUBLANES = 8
FRONT = 256
FIRST_VALID = FRONT - N_META
ATT_TILE = 256
ATT_PAIRS = 4
HG_CHUNK = 128
HG_LEVELS = 7
HG_STEP_CHUNKS = 2
PROJ_ROWS = 256
FFN_ROWS = 512
FFN_SPLIT = 2
VMEM_LIMIT = 56 * 1024 * 1024
Q_SCALE = float(np.log2(np.e)) * SB_HEAD_DIM ** -0.5
MASKED = -1e9
SP_LINEAR = 64.0
EXIT_SUM = 160.0

NT_DIMS = (((1,), (1,)), ((), ()))
TN_DIMS = (((0,), (0,)), ((), ()))


def _const_spec(shape):
    return pl.BlockSpec(shape, lambda *_: (0,) * len(shape))


def _inproj_kernel(x_ref, head_ref, g_ref, w32_ref, qkv_ref, hq_ref, hf_ref, hi_ref, hg_ref,
                   w_ref, u_ref):
    j = pl.program_id(1)

    def normed(h):
        u = h * lax.rsqrt(jnp.mean(h * h, axis=-1, keepdims=True) + RMS_EPS) * g_ref[...]
        return u.astype(BF16)

    @pl.when(j == 0)
    def _():
        w_ref[...] = w32_ref[...].astype(BF16)
        u_ref[0] = normed(head_ref[...])

    ub = u_ref[j % 2]

    def proj(lo, width):
        return jnp.dot(ub, w_ref[:, lo:lo + width], preferred_element_type=F32)

    qkv_ref[0, :, 0:SB_WIDTH] = (proj(0, SB_WIDTH) * Q_SCALE).astype(BF16)
    qkv_ref[0, :, SB_WIDTH:3 * SB_WIDTH] = proj(SB_WIDTH, 2 * SB_WIDTH).astype(BF16)
    base = 3 * SB_WIDTH
    hq_ref[0] = proj(base, HG_WIDTH).astype(BF16)
    hf_ref[0] = proj(base + HG_WIDTH, HG_WIDTH)
    hi_ref[0] = proj(base + 2 * HG_WIDTH, HG_WIDTH).astype(BF16)
    hg_ref[0] = proj(base + 3 * HG_WIDTH, HG_WIDTH)
    u_ref[(j + 1) % 2] = normed(x_ref[0])


def _input_projection(x, head, norm_g, w_in):
    batch, seq, _ = x.shape
    n_tiles = (FRONT + seq) // PROJ_ROWS
    rows = FRONT + seq
    row_spec = lambda width: pl.BlockSpec((1, PROJ_ROWS, width), lambda b, j: (b, j, 0))
    return pl.pallas_call(
        _inproj_kernel,
        grid=(batch, n_tiles),
        in_specs=[
            pl.BlockSpec((1, PROJ_ROWS, D_MODEL), lambda b, j: (b, jnp.minimum(j, n_tiles - 2), 0)),
            _const_spec((PROJ_ROWS, D_MODEL)),
            _const_spec((1, D_MODEL)),
            pl.BlockSpec((D_MODEL, IN_WIDTH), lambda b, j: (0, 0), pipeline_mode=pl.Buffered(1)),
        ],
        out_specs=[row_spec(3 * SB_WIDTH), row_spec(HG_WIDTH), row_spec(HG_WIDTH),
                   row_spec(HG_WIDTH), row_spec(HG_WIDTH)],
        out_shape=[
            jax.ShapeDtypeStruct((batch, rows, 3 * SB_WIDTH), BF16),
            jax.ShapeDtypeStruct((batch, rows, HG_WIDTH), BF16),
            jax.ShapeDtypeStruct((batch, rows, HG_WIDTH), F32),
            jax.ShapeDtypeStruct((batch, rows, HG_WIDTH), BF16),
            jax.ShapeDtypeStruct((batch, rows, HG_WIDTH), F32),
        ],
        scratch_shapes=[pltpu.VMEM((D_MODEL, IN_WIDTH), BF16),
                        pltpu.VMEM((2, PROJ_ROWS, D_MODEL), BF16)],
        compiler_params=pltpu.CompilerParams(
            dimension_semantics=("arbitrary", "arbitrary"), vmem_limit_bytes=VMEM_LIMIT),
        name="in_projection",
    )(x, head, norm_g, w_in)


def _attn_kernel(q_ref, k_ref, v_ref, tri_ref, g_ref, o_ref, qh_ref, z_ref, c_ref, sum_ref, acc_ref):
    tile = ATT_TILE
    n_heads = qh_ref.shape[0]
    i = pl.program_id(2) + FRONT // ATT_TILE
    last = i - FIRST_VALID // ATT_TILE
    lane = lax.broadcasted_iota(jnp.int32, (1, LANES), 1)
    first_head = lane < SB_HEAD_DIM
    t_pos = i * tile + lax.broadcasted_iota(jnp.int32, (1, tile), 1)

    def pair_lanes(h):
        return slice((h // 2) * LANES, (h // 2 + 1) * LANES)

    def rows_of(c):
        return pl.ds(pl.multiple_of(c * tile, tile), tile)

    def minus_row(a, row8):
        return (a.reshape(tile // SUBLANES, SUBLANES, tile) - row8[None]).reshape(tile, tile)

    def process(positions, from_diagonal=False):
        half = tile // 2

        def on_visible_quadrants(fn, *arrays):
            top = fn(*[a[0:half, :] for a in arrays])
            corner = fn(*[a[half:, half:] for a in arrays])
            return jnp.concatenate(
                [top, jnp.concatenate([jnp.zeros_like(corner), corner], axis=1)], axis=0)

        def softplus_bf16(z):
            zb = z.astype(BF16)
            grown = 1.0 + jnp.exp2(jnp.minimum(zb, SP_LINEAR))
            return jnp.maximum(zb, jnp.log2(grown.astype(F32)).astype(BF16))

        def weight_bf16(y, local):
            return jnp.exp2((y - local).astype(BF16))
        visible = []
        for n in positions:
            s_pos = (i - n) * tile + lax.broadcasted_iota(jnp.int32, (tile, 1), 0)
            visible.append((s_pos < t_pos) & (s_pos >= FIRST_VALID))

        def scores(h):
            for slot, n in enumerate(positions):
                z = lax.dot_general(k_ref[0, rows_of(i - n), pair_lanes(h)], qh_ref[h], NT_DIMS,
                                    preferred_element_type=F32)
                z_ref[slot, h] = jnp.where(visible[slot], z, MASKED)

        def suffix_sums(h):
            for slot in range(len(positions)):
                if from_diagonal and slot == 0:
                    sp = on_visible_quadrants(softplus_bf16, z_ref[slot, h])
                else:
                    sp = softplus_bf16(z_ref[slot, h])
                c_ref[slot, h] = jnp.dot(tri_ref[...], sp, preferred_element_type=F32)

        def weights(h):
            later = sum_ref[h]
            acc = acc_ref[h]
            for slot, n in enumerate(positions):
                local = c_ref[slot, h]
                if from_diagonal and slot == 0:
                    w = on_visible_quadrants(weight_bf16, z_ref[slot, h], local)
                else:
                    w = weight_bf16(minus_row(z_ref[slot, h], later), local)
                acc = acc + lax.dot_general(v_ref[0, rows_of(i - n), pair_lanes(h)], w, TN_DIMS,
                                            preferred_element_type=F32)
                later = later + jnp.broadcast_to(local[0:1, :], (SUBLANES, tile))
            sum_ref[h] = later
            acc_ref[h] = acc

        phases = (scores, suffix_sums, weights)
        if len(positions) == 1:
            for phase in phases:
                for h in range(n_heads):
                    phase(h)
            return
        for k in range(n_heads + len(phases) - 1):
            for lag, phase in enumerate(phases):
                if 0 <= k - lag < n_heads:
                    phase(k - lag)

    for h in range(n_heads):
        q_pair = q_ref[0, :, pair_lanes(h)]
        qh_ref[h] = jnp.where(first_head == (h % 2 == 0), q_pair, jnp.zeros_like(q_pair))
    sum_ref[...] = jnp.zeros(sum_ref.shape, F32)
    acc_ref[...] = jnp.zeros(acc_ref.shape, F32)
    process([0, 1], from_diagonal=True)

    def sweep(carry):
        n, _ = carry
        process([n])
        return n + 1, jnp.min(sum_ref[...])

    lax.while_loop(lambda carry: (carry[0] <= last) & (carry[1] < EXIT_SUM),
                   sweep, (jnp.int32(2), jnp.min(sum_ref[...])))

    for p in range(n_heads // 2):
        o = jnp.concatenate([acc_ref[2 * p, 0:SB_HEAD_DIM], acc_ref[2 * p + 1, SB_HEAD_DIM:LANES]],
                            axis=0).T
        sq = o * o
        ss_first = jnp.sum(jnp.where(first_head, sq, 0.0), axis=-1, keepdims=True)
        ss_second = jnp.sum(sq, axis=-1, keepdims=True) - ss_first
        mean_sq = jnp.where(first_head, ss_first, ss_second) * (1.0 / SB_HEAD_DIM)
        lanes = slice(p * LANES, (p + 1) * LANES)
        o_ref[0, :, lanes] = (o * lax.rsqrt(mean_sq + RMS_EPS) * g_ref[:, lanes]).astype(o_ref.dtype)


def _reverse_cumsum_matrix(n):
    j = np.arange(n)[:, None]
    s = np.arange(n)[None, :]
    return jnp.asarray((s >= j).astype(np.float32), BF16)


def _attention(qkv, sb_norm_g, seq):
    batch, rows, _ = qkv.shape
    width = ATT_PAIRS * LANES
    groups = SB_WIDTH // width
    n_q = seq // ATT_TILE
    heads = 2 * ATT_PAIRS
    return pl.pallas_call(
        _attn_kernel,
        grid=(batch, groups, n_q),
        in_specs=[
            pl.BlockSpec((1, ATT_TILE, width), lambda b, g, i: (b, i + FRONT // ATT_TILE, g)),
            pl.BlockSpec((1, rows, width), lambda b, g, i: (b, 0, groups + g)),
            pl.BlockSpec((1, rows, width), lambda b, g, i: (b, 0, 2 * groups + g)),
            _const_spec((ATT_TILE, ATT_TILE)),
            pl.BlockSpec((1, width), lambda b, g, i: (0, g)),
        ],
        out_specs=pl.BlockSpec((1, ATT_TILE, width), lambda b, g, i: (b, i, g)),
        out_shape=jax.ShapeDtypeStruct((batch, seq, SB_WIDTH), BF16),
        scratch_shapes=[pltpu.VMEM((heads, ATT_TILE, LANES), BF16),
                        pltpu.VMEM((2, heads, ATT_TILE, ATT_TILE), F32),
                        pltpu.VMEM((2, heads, ATT_TILE, ATT_TILE), F32),
                        pltpu.VMEM((heads, SUBLANES, ATT_TILE), F32),
                        pltpu.VMEM((heads, LANES, ATT_TILE), F32)],
        compiler_params=pltpu.CompilerParams(
            dimension_semantics=("parallel", "parallel", "parallel"), vmem_limit_bytes=VMEM_LIMIT),
        name="stickbreak_attention",
    )(qkv, qkv, qkv, _reverse_cumsum_matrix(ATT_TILE), sb_norm_g)


def _hgrn_exponent_matrix():
    c = HG_CHUNK
    t = np.arange(c)[:, None]
    j = np.arange(c)[None, :]
    mats = [(j <= t)]
    for lvl in range(HG_LEVELS):
        m = c >> (lvl + 1)
        blk = t // m
        odd = (blk % 2) == 1
        prefix = (j >= blk * m) & (j <= t)
        suffix = (j > t) & (j <= blk * m + m - 1)
        mats.append(np.where(odd, prefix, suffix))
    z = np.concatenate(mats, axis=0).astype(np.float32)
    return jnp.asarray(np.concatenate([z, z], axis=1), BF16)


def _hgrn_level_masks():
    c = HG_CHUNK
    t = np.arange(c)[:, None]
    s = np.arange(c)[None, :]
    masks = [t == s]
    for lvl in range(HG_LEVELS):
        m = c >> (lvl + 1)
        masks.append(((t // m) == (s // m) + 1) & (((s // m) % 2) == 0))
    return jnp.asarray(np.stack(masks).astype(np.float32))


def _hgrn_kernel(q_ref, f_ref, i_ref, gate_ref, lbl_ref, ng_ref, zmat_ref, mask_ref, o_ref,
                 state_ref):
    j = pl.program_id(1)
    ch = HG_CHUNK
    rows = HG_STEP_CHUNKS * ch

    @pl.when(j == 0)
    def _():
        state_ref[...] = jnp.zeros_like(state_ref)

    logits = lbl_ref[...]
    mx = jnp.max(logits, axis=0, keepdims=True)
    ex = jnp.exp(logits - mx)
    lb = ex[0:1] / jnp.sum(ex, axis=0, keepdims=True)

    f_raw = f_ref[0]
    sig = 0.5 + 0.5 * jnp.tanh(0.5 * f_raw)
    log_f = jnp.log2(lb + (1.0 - lb) * sig)
    kk = (1.0 - lb) * (1.0 - sig)
    row_all = lax.broadcasted_iota(jnp.int32, (rows, 1), 0)
    valid = (j > 0) | (row_all >= FIRST_VALID)
    log_f = jnp.where(valid, log_f, 0.0)
    kk = jnp.where(valid, kk, 0.0)
    hi = log_f.astype(BF16)
    lo = (log_f - hi.astype(F32)).astype(BF16)
    q_all = q_ref[0].astype(F32)
    v_all = i_ref[0]

    chunks = [slice(n * ch, (n + 1) * ch) for n in range(HG_STEP_CHUNKS)]
    heads = [slice(h * HG_DIM, (h + 1) * HG_DIM) for h in range(HG_HEADS)]
    row = lax.broadcasted_iota(jnp.int32, (ch, 1), 0)
    expo, sides, q_bf, k_bf, q_inter, k_state, state_decay = {}, {}, {}, {}, {}, {}, {}
    parts, scores, intra, kv = {}, {}, {}, {}

    def exponents(n):
        r = chunks[n]
        expo[n] = jnp.dot(zmat_ref[...], jnp.concatenate([hi[r], lo[r]], axis=0),
                          preferred_element_type=F32)

    def decayed_operands(n):
        q, k = q_all[chunks[n]], kk[chunks[n]]
        level_sides = []
        for lvl in range(HG_LEVELS):
            m = ch >> (lvl + 1)
            decay = jnp.exp2(expo[n][(lvl + 1) * ch:(lvl + 2) * ch])
            if m % SUBLANES == 0:
                both = jnp.concatenate([(q if blk % 2 else k)[blk * m:(blk + 1) * m]
                                        for blk in range(ch // m)], axis=0)
            else:
                both = jnp.where(((row // m) % 2) == 1, q, k)
            level_sides.append((both * decay).astype(BF16))
        sides[n] = level_sides
        q_bf[n] = q.astype(BF16)
        k_bf[n] = k.astype(BF16)
        b = expo[n][0:ch]
        b_last = b[ch - 1:ch]
        q_inter[n] = (q * jnp.exp2(b)).astype(BF16)
        k_state[n] = (k * jnp.exp2(b_last - b)).astype(BF16)
        state_decay[n] = jnp.exp2(b_last)

    def level_products(n):
        parts[n] = [[lax.dot_general(x[:, sl], y[:, sl], NT_DIMS, preferred_element_type=F32)
                     for x, y in [(q_bf[n], k_bf[n])] + [(sd, sd) for sd in sides[n]]]
                    for sl in heads]

    def assemble_scores(n):
        per_head = []
        for h in range(HG_HEADS):
            sc = parts[n][h][0] * mask_ref[0]
            for lvl in range(HG_LEVELS):
                sc = sc + parts[n][h][lvl + 1] * mask_ref[lvl + 1]
            per_head.append(sc.astype(BF16))
        scores[n] = per_head

    def value_products(n):
        r = chunks[n]
        intra[n] = [jnp.dot(scores[n][h], v_all[r, sl], preferred_element_type=F32)
                    for h, sl in enumerate(heads)]
        kv[n] = [lax.dot_general(v_all[r, sl], k_state[n][:, sl], TN_DIMS,
                                 preferred_element_type=F32) for sl in heads]

    phases = (exponents, decayed_operands, level_products, assemble_scores, value_products)
    for k in range(HG_STEP_CHUNKS + len(phases) - 1):
        for lag, phase in enumerate(phases):
            if 0 <= k - lag < HG_STEP_CHUNKS:
                phase(k - lag)

    for n, r in enumerate(chunks):
        outs = []
        for h, sl in enumerate(heads):
            st = state_ref[h]
            o_h = intra[n][h] + lax.dot_general(q_inter[n][:, sl], st.astype(BF16), NT_DIMS,
                                                preferred_element_type=F32)
            state_ref[h] = st * state_decay[n][:, sl] + kv[n][h]
            outs.append(o_h * lax.rsqrt(jnp.mean(o_h * o_h, axis=-1, keepdims=True) + RMS_EPS))
        o = jnp.concatenate(outs, axis=1) * ng_ref[...]
        half_gate = 0.5 * gate_ref[0, r, :]
        o_ref[0, r, :] = (o * (half_gate + half_gate * jnp.tanh(half_gate))).astype(o_ref.dtype)


def _hgrn(hq, hf, hi, hgate, lb_logits, hg_norm_g, seq):
    batch = hq.shape[0]
    rows = HG_STEP_CHUNKS * HG_CHUNK
    assert FRONT == rows, "the first grid step must hold exactly the rows in front of x"
    in_spec = pl.BlockSpec((1, rows, HG_WIDTH), lambda b, j: (b, j, 0))
    zmat = _hgrn_exponent_matrix()
    masks = _hgrn_level_masks()
    return pl.pallas_call(
        _hgrn_kernel,
        grid=(batch, (FRONT + seq) // rows),
        in_specs=[in_spec, in_spec, in_spec, in_spec,
                  _const_spec(lb_logits.shape), _const_spec((1, HG_WIDTH)),
                  _const_spec(zmat.shape), _const_spec(masks.shape)],
        out_specs=pl.BlockSpec((1, rows, HG_WIDTH), lambda b, j: (b, jnp.maximum(j - 1, 0), 0)),
        out_shape=jax.ShapeDtypeStruct((batch, seq, HG_WIDTH), BF16),
        scratch_shapes=[pltpu.VMEM((HG_HEADS, HG_DIM, HG_DIM), F32)],
        compiler_params=pltpu.CompilerParams(
            dimension_semantics=("parallel", "arbitrary"), vmem_limit_bytes=VMEM_LIMIT),
        name="hgrn2",
    )(hq, hf, hi, hgate, lb_logits, hg_norm_g, zmat, masks)


def _ffn_kernel(x_ref, osb_ref, ohg_ref, wout_ref, g2_ref, wg_ref, wu_ref, wd_ref, gf_ref, o_ref):
    half = FFN_ROWS // 2
    width = D_FF // FFN_SPLIT
    h1, u2, ffn = {}, {}, {}

    def mix_and_norm(r):
        rows = slice(r * half, (r + 1) * half)
        mix = (jnp.dot(osb_ref[0, rows, :], wout_ref[0:SB_WIDTH, :], preferred_element_type=F32)
               + jnp.dot(ohg_ref[0, rows, :], wout_ref[SB_WIDTH:, :], preferred_element_type=F32))
        h1[r] = x_ref[0, rows, :] + mix
        u2[r] = (h1[r] * lax.rsqrt(jnp.mean(h1[r] * h1[r], axis=-1, keepdims=True) + RMS_EPS)
                 * g2_ref[...]).astype(BF16)

    def swiglu(r):
        acc = jnp.zeros_like(h1[r])
        for n in range(FFN_SPLIT):
            cols = slice(n * width, (n + 1) * width)
            gate = jnp.dot(u2[r], wg_ref[:, cols], preferred_element_type=F32)
            up = jnp.dot(u2[r], wu_ref[:, cols], preferred_element_type=F32)
            half_gate = 0.5 * gate
            act = ((half_gate + half_gate * jnp.tanh(half_gate)) * up).astype(BF16)
            acc = acc + jnp.dot(act, wd_ref[cols, :], preferred_element_type=F32)
        ffn[r] = acc

    def final_norm(r):
        rows = slice(r * half, (r + 1) * half)
        h2 = h1[r] + ffn[r]
        o_ref[0, rows, :] = (h2 * lax.rsqrt(jnp.mean(h2 * h2, axis=-1, keepdims=True) + RMS_EPS)
                             * gf_ref[...])

    phases = (mix_and_norm, swiglu, final_norm)
    for k in range(2 + len(phases) - 1):
        for lag, phase in enumerate(phases):
            if 0 <= k - lag < 2:
                phase(k - lag)


def _out_projection_ffn(x, o_sb, o_hg, w_out, norm2_g, w_gate, w_up, w_down, final_g):
    batch, seq, _ = x.shape
    row_spec = lambda width: pl.BlockSpec((1, FFN_ROWS, width), lambda b, j: (b, j, 0))
    resident = lambda shape: pl.BlockSpec(shape, lambda b, j: (0, 0), pipeline_mode=pl.Buffered(1))
    return pl.pallas_call(
        _ffn_kernel,
        grid=(batch, seq // FFN_ROWS),
        in_specs=[row_spec(D_MODEL), row_spec(SB_WIDTH), row_spec(HG_WIDTH),
                  resident((D_MODEL, D_MODEL)), resident((1, D_MODEL)),
                  resident((D_MODEL, D_FF)), resident((D_MODEL, D_FF)), resident((D_FF, D_MODEL)),
                  resident((1, D_MODEL))],
        out_specs=row_spec(D_MODEL),
        out_shape=jax.ShapeDtypeStruct((batch, seq, D_MODEL), x.dtype),
        compiler_params=pltpu.CompilerParams(
            dimension_semantics=("parallel", "parallel"), vmem_limit_bytes=VMEM_LIMIT),
        name="out_projection_ffn",
    )(x, o_sb, o_hg, w_out, norm2_g, w_gate, w_up, w_down, final_g)


def kernel(x, meta_tokens, norm1_g, w_in, sb_norm_g, hg_norm_g, hg_lb_logits, w_out, norm2_g,
           w_gate, w_up, w_down, final_norm_g):
    batch, seq, d_model = x.shape
    assert d_model == D_MODEL and seq % FFN_ROWS == 0 and seq % ATT_TILE == 0
    assert norm1_g.shape[0] == 1, "single-layer block"
    head = jnp.concatenate(
        [jnp.zeros((FIRST_VALID, D_MODEL), x.dtype), meta_tokens.astype(x.dtype)], axis=0)
    row = lambda g: g.reshape(1, -1).astype(F32)

    qkv, hq, hf, hi, hgate = _input_projection(x, head, row(norm1_g[0]), w_in[0].astype(F32))
    o_sb = _attention(qkv, row(sb_norm_g[0]), seq)
    o_hg = _hgrn(hq, hf, hi, hgate, hg_lb_logits.astype(F32), row(hg_norm_g[0]), seq)
    return _out_projection_ffn(
        x, o_sb, o_hg, w_out[0].astype(BF16), row(norm2_g[0]), w_gate[0].astype(BF16),
        w_up[0].astype(BF16), w_down[0].astype(BF16), row(final_norm_g))
```

```python
import functools

import jax
import jax.numpy as jnp
import numpy as np
from jax import lax
from jax.experimental import pallas as pl
from jax.experimental.pallas import tpu as pltpu

F32 = jnp.float32
BF16 = jnp.bfloat16

D_MODEL = 1024
N_META = 16
SB_HEADS = 8
SB_HEAD_DIM = 64
SB_WIDTH = SB_HEADS * SB_HEAD_DIM
HG_HEADS = 4
HG_DIM = 128
HG_WIDTH = HG_HEADS * HG_DIM
IN_WIDTH = 3 * SB_WIDTH + 4 * HG_WIDTH
D_FF = 2816
RMS_EPS = 1e-6

LANES = 128
SUBLANES = 8
FRONT = 256
FIRST_VALID = FRONT - N_META
ATT_TILE = 256
ATT_PAIRS = 4
HG_CHUNK = 128
HG_LEVELS = 7
HG_STEP_CHUNKS = 2
PROJ_ROWS = 256
FFN_ROWS = 512
FFN_SPLIT = 1
VMEM_LIMIT = 56 * 1024 * 1024
Q_SCALE = float(np.log2(np.e)) * SB_HEAD_DIM ** -0.5
MASKED = -1e9
SP_LINEAR = 64.0
EXIT_SUM = 160.0

NT_DIMS = (((1,), (1,)), ((), ()))
TN_DIMS = (((0,), (0,)), ((), ()))


def _const_spec(shape):
    return pl.BlockSpec(shape, lambda *_: (0,) * len(shape))


def _inproj_kernel(x_ref, head_ref, g_ref, w32_ref, qkv_ref, hq_ref, hf_ref, hi_ref, hg_ref,
                   w_ref, u_ref):
    j = pl.program_id(1)

    def normed(h):
        u = h * lax.rsqrt(jnp.mean(h * h, axis=-1, keepdims=True) + RMS_EPS) * g_ref[...]
        return u.astype(BF16)

    @pl.when(j == 0)
    def _():
        w_ref[...] = w32_ref[...].astype(BF16)
        u_ref[0] = normed(head_ref[...])

    ub = u_ref[j % 2]

    def proj(lo, width):
        return jnp.dot(ub, w_ref[:, lo:lo + width], preferred_element_type=F32)

    qkv_ref[0, :, 0:SB_WIDTH] = (proj(0, SB_WIDTH) * Q_SCALE).astype(BF16)
    qkv_ref[0, :, SB_WIDTH:3 * SB_WIDTH] = proj(SB_WIDTH, 2 * SB_WIDTH).astype(BF16)
    base = 3 * SB_WIDTH
    hq_ref[0] = proj(base, HG_WIDTH).astype(BF16)
    hf_ref[0] = proj(base + HG_WIDTH, HG_WIDTH)
    hi_ref[0] = proj(base + 2 * HG_WIDTH, HG_WIDTH).astype(BF16)
    hg_ref[0] = proj(base + 3 * HG_WIDTH, HG_WIDTH)
    u_ref[(j + 1) % 2] = normed(x_ref[0])


def _input_projection(x, head, norm_g, w_in):
    batch, seq, _ = x.shape
    n_tiles = (FRONT + seq) // PROJ_ROWS
    rows = FRONT + seq
    row_spec = lambda width: pl.BlockSpec((1, PROJ_ROWS, width), lambda b, j: (b, j, 0))
    return pl.pallas_call(
        _inproj_kernel,
        grid=(batch, n_tiles),
        in_specs=[
            pl.BlockSpec((1, PROJ_ROWS, D_MODEL), lambda b, j: (b, jnp.minimum(j, n_tiles - 2), 0)),
            _const_spec((PROJ_ROWS, D_MODEL)),
            _const_spec((1, D_MODEL)),
            pl.BlockSpec((D_MODEL, IN_WIDTH), lambda b, j: (0, 0), pipeline_mode=pl.Buffered(1)),
        ],
        out_specs=[row_spec(3 * SB_WIDTH), row_spec(HG_WIDTH), row_spec(HG_WIDTH),
                   row_spec(HG_WIDTH), row_spec(HG_WIDTH)],
        out_shape=[
            jax.ShapeDtypeStruct((batch, rows, 3 * SB_WIDTH), BF16),
            jax.ShapeDtypeStruct((batch, rows, HG_WIDTH), BF16),
            jax.ShapeDtypeStruct((batch, rows, HG_WIDTH), F32),
            jax.ShapeDtypeStruct((batch, rows, HG_WIDTH), BF16),
            jax.ShapeDtypeStruct((batch, rows, HG_WIDTH), F32),
        ],
        scratch_shapes=[pltpu.VMEM((D_MODEL, IN_WIDTH), BF16),
                        pltpu.VMEM((2, PROJ_ROWS, D_MODEL), BF16)],
        compiler_params=pltpu.CompilerParams(
            dimension_semantics=("arbitrary", "arbitrary"), vmem_limit_bytes=VMEM_LIMIT),
        name="in_projection",
    )(x, head, norm_g, w_in)


def _attn_kernel(q_ref, k_ref, v_ref, tri_ref, g_ref, o_ref, qh_ref, z_ref, c_ref, sum_ref, acc_ref):
    tile = ATT_TILE
    n_heads = qh_ref.shape[0]
    i = pl.program_id(2) + FRONT // ATT_TILE
    last = i - FIRST_VALID // ATT_TILE
    lane = lax.broadcasted_iota(jnp.int32, (1, LANES), 1)
    first_head = lane < SB_HEAD_DIM
    t_pos = i * tile + lax.broadcasted_iota(jnp.int32, (1, tile), 1)

    def pair_lanes(h):
        return slice((h // 2) * LANES, (h // 2 + 1) * LANES)

    def rows_of(c):
        return pl.ds(pl.multiple_of(c * tile, tile), tile)

    def minus_row(a, row8):
        return (a.reshape(tile // SUBLANES, SUBLANES, tile) - row8[None]).reshape(tile, tile)

    def process(positions, from_diagonal=False):
        half = tile // 2

        def on_visible_quadrants(fn, *arrays):
            top = fn(*[a[0:half, :] for a in arrays])
            corner = fn(*[a[half:, half:] for a in arrays])
            return jnp.concatenate(
                [top, jnp.concatenate([jnp.zeros_like(corner), corner], axis=1)], axis=0)

        def softplus_bf16(z):
            zb = z.astype(BF16)
            grown = 1.0 + jnp.exp2(jnp.minimum(zb, SP_LINEAR))
            return jnp.maximum(zb, jnp.log2(grown.astype(F32)).astype(BF16))

        def weight_bf16(y, local):
            return jnp.exp2((y - local).astype(BF16))
        visible = []
        for n in positions:
            s_pos = (i - n) * tile + lax.broadcasted_iota(jnp.int32, (tile, 1), 0)
            visible.append((s_pos < t_pos) & (s_pos >= FIRST_VALID))

        def scores(h):
            for slot, n in enumerate(positions):
                z = lax.dot_general(k_ref[0, rows_of(i - n), pair_lanes(h)], qh_ref[h], NT_DIMS,
                                    preferred_element_type=F32)
                z_ref[slot, h] = jnp.where(visible[slot], z, MASKED)

        def suffix_sums(h):
            for slot in range(len(positions)):
                if from_diagonal and slot == 0:
                    sp = on_visible_quadrants(softplus_bf16, z_ref[slot, h])
                else:
                    sp = softplus_bf16(z_ref[slot, h])
                c_ref[slot, h] = jnp.dot(tri_ref[...], sp, preferred_element_type=F32)

        def weights(h):
            later = sum_ref[h]
            acc = acc_ref[h]
            for slot, n in enumerate(positions):
                local = c_ref[slot, h]
                if from_diagonal and slot == 0:
                    w = on_visible_quadrants(weight_bf16, z_ref[slot, h], local)
                else:
                    w = weight_bf16(minus_row(z_ref[slot, h], later), local)
                acc = acc + lax.dot_general(v_ref[0, rows_of(i - n), pair_lanes(h)], w, TN_DIMS,
                                            preferred_element_type=F32)
                later = later + jnp.broadcast_to(local[0:1, :], (SUBLANES, tile))
            sum_ref[h] = later
            acc_ref[h] = acc

        phases = (scores, suffix_sums, weights)
        if len(positions) == 1:
            for phase in phases:
                for h in range(n_heads):
                    phase(h)
            return
        for k in range(n_heads + len(phases) - 1):
            for lag, phase in enumerate(phases):
                if 0 <= k - lag < n_heads:
                    phase(k - lag)

    for h in range(n_heads):
        q_pair = q_ref[0, :, pair_lanes(h)]
        qh_ref[h] = jnp.where(first_head == (h % 2 == 0), q_pair, jnp.zeros_like(q_pair))
    sum_ref[...] = jnp.zeros(sum_ref.shape, F32)
    acc_ref[...] = jnp.zeros(acc_ref.shape, F32)
    process([0, 1], from_diagonal=True)

    def sweep(carry):
        n, _ = carry
        process([n])
        return n + 1, jnp.min(sum_ref[...])

    lax.while_loop(lambda carry: (carry[0] <= last) & (carry[1] < EXIT_SUM),
                   sweep, (jnp.int32(2), jnp.min(sum_ref[...])))

    for p in range(n_heads // 2):
        o = jnp.concatenate([acc_ref[2 * p, 0:SB_HEAD_DIM], acc_ref[2 * p + 1, SB_HEAD_DIM:LANES]],
                            axis=0).T
        sq = o * o
        ss_first = jnp.sum(jnp.where(first_head, sq, 0.0), axis=-1, keepdims=True)
        ss_second = jnp.sum(sq, axis=-1, keepdims=True) - ss_first
        mean_sq = jnp.where(first_head, ss_first, ss_second) * (1.0 / SB_HEAD_DIM)
        lanes = slice(p * LANES, (p + 1) * LANES)
        o_ref[0, :, lanes] = (o * lax.rsqrt(mean_sq + RMS_EPS) * g_ref[:, lanes]).astype(o_ref.dtype)


def _reverse_cumsum_matrix(n):
    j = np.arange(n)[:, None]
    s = np.arange(n)[None, :]
    return jnp.asarray((s >= j).astype(np.float32), BF16)


def _attention(qkv, sb_norm_g, seq):
    batch, rows, _ = qkv.shape
    width = ATT_PAIRS * LANES
    groups = SB_WIDTH // width
    n_q = seq // ATT_TILE
    heads = 2 * ATT_PAIRS
    return pl.pallas_call(
        _attn_kernel,
        grid=(batch, groups, n_q),
        in_specs=[
            pl.BlockSpec((1, ATT_TILE, width), lambda b, g, i: (b, i + FRONT // ATT_TILE, g)),
            pl.BlockSpec((1, rows, width), lambda b, g, i: (b, 0, groups + g)),
            pl.BlockSpec((1, rows, width), lambda b, g, i: (b, 0, 2 * groups + g)),
            _const_spec((ATT_TILE, ATT_TILE)),
            pl.BlockSpec((1, width), lambda b, g, i: (0, g)),
        ],
        out_specs=pl.BlockSpec((1, ATT_TILE, width), lambda b, g, i: (b, i, g)),
        out_shape=jax.ShapeDtypeStruct((batch, seq, SB_WIDTH), BF16),
        scratch_shapes=[pltpu.VMEM((heads, ATT_TILE, LANES), BF16),
                        pltpu.VMEM((2, heads, ATT_TILE, ATT_TILE), F32),
                        pltpu.VMEM((2, heads, ATT_TILE, ATT_TILE), F32),
                        pltpu.VMEM((heads, SUBLANES, ATT_TILE), F32),
                        pltpu.VMEM((heads, LANES, ATT_TILE), F32)],
        compiler_params=pltpu.CompilerParams(
            dimension_semantics=("parallel", "parallel", "parallel"), vmem_limit_bytes=VMEM_LIMIT),
        name="stickbreak_attention",
    )(qkv, qkv, qkv, _reverse_cumsum_matrix(ATT_TILE), sb_norm_g)


def _hgrn_exponent_matrix():
    c = HG_CHUNK
    t = np.arange(c)[:, None]
    j = np.arange(c)[None, :]
    mats = [(j <= t)]
    for lvl in range(HG_LEVELS):
        m = c >> (lvl + 1)
        blk = t // m
        odd = (blk % 2) == 1
        prefix = (j >= blk * m) & (j <= t)
        suffix = (j > t) & (j <= blk * m + m - 1)
        mats.append(np.where(odd, prefix, suffix))
    z = np.concatenate(mats, axis=0).astype(np.float32)
    return jnp.asarray(np.concatenate([z, z], axis=1), BF16)


def _hgrn_level_masks():
    c = HG_CHUNK
    t = np.arange(c)[:, None]
    s = np.arange(c)[None, :]
    masks = [t == s]
    for lvl in range(HG_LEVELS):
        m = c >> (lvl + 1)
        masks.append(((t // m) == (s // m) + 1) & (((s // m) % 2) == 0))
    return jnp.asarray(np.stack(masks).astype(np.float32))


def _hgrn_kernel(q_ref, f_ref, i_ref, gate_ref, lbl_ref, ng_ref, zmat_ref, mask_ref, o_ref,
                 state_ref):
    j = pl.program_id(1)
    ch = HG_CHUNK
    rows = HG_STEP_CHUNKS * ch

    @pl.when(j == 0)
    def _():
        state_ref[...] = jnp.zeros_like(state_ref)

    logits = lbl_ref[...]
    mx = jnp.max(logits, axis=0, keepdims=True)
    ex = jnp.exp(logits - mx)
    lb = ex[0:1] / jnp.sum(ex, axis=0, keepdims=True)

    f_raw = f_ref[0]
    sig = 0.5 + 0.5 * jnp.tanh(0.5 * f_raw)
    log_f = jnp.log2(lb + (1.0 - lb) * sig)
    kk = (1.0 - lb) * (1.0 - sig)
    row_all = lax.broadcasted_iota(jnp.int32, (rows, 1), 0)
    valid = (j > 0) | (row_all >= FIRST_VALID)
    log_f = jnp.where(valid, log_f, 0.0)
    kk = jnp.where(valid, kk, 0.0)
    hi = log_f.astype(BF16)
    lo = (log_f - hi.astype(F32)).astype(BF16)
    q_all = q_ref[0].astype(F32)
    v_all = i_ref[0]

    chunks = [slice(n * ch, (n + 1) * ch) for n in range(HG_STEP_CHUNKS)]
    heads = [slice(h * HG_DIM, (h + 1) * HG_DIM) for h in range(HG_HEADS)]
    row = lax.broadcasted_iota(jnp.int32, (ch, 1), 0)
    expo, sides, q_bf, k_bf, q_inter, k_state, state_decay = {}, {}, {}, {}, {}, {}, {}
    parts, scores, intra, kv = {}, {}, {}, {}

    def exponents(n):
        r = chunks[n]
        expo[n] = jnp.dot(zmat_ref[...], jnp.concatenate([hi[r], lo[r]], axis=0),
                          preferred_element_type=F32)

    def decayed_operands(n):
        q, k = q_all[chunks[n]], kk[chunks[n]]
        level_sides = []
        for lvl in range(HG_LEVELS):
            m = ch >> (lvl + 1)
            decay = jnp.exp2(expo[n][(lvl + 1) * ch:(lvl + 2) * ch])
            if m % SUBLANES == 0:
                both = jnp.concatenate([(q if blk % 2 else k)[blk * m:(blk + 1) * m]
                                        for blk in range(ch // m)], axis=0)
            else:
                both = jnp.where(((row // m) % 2) == 1, q, k)
            level_sides.append((both * decay).astype(BF16))
        sides[n] = level_sides
        q_bf[n] = q.astype(BF16)
        k_bf[n] = k.astype(BF16)
        b = expo[n][0:ch]
        b_last = b[ch - 1:ch]
        q_inter[n] = (q * jnp.exp2(b)).astype(BF16)
        k_state[n] = (k * jnp.exp2(b_last - b)).astype(BF16)
        state_decay[n] = jnp.exp2(b_last)

    def level_products(n):
        parts[n] = [[lax.dot_general(x[:, sl], y[:, sl], NT_DIMS, preferred_element_type=F32)
                     for x, y in [(q_bf[n], k_bf[n])] + [(sd, sd) for sd in sides[n]]]
                    for sl in heads]

    def assemble_scores(n):
        per_head = []
        for h in range(HG_HEADS):
            sc = parts[n][h][0] * mask_ref[0]
            for lvl in range(HG_LEVELS):
                sc = sc + parts[n][h][lvl + 1] * mask_ref[lvl + 1]
            per_head.append(sc.astype(BF16))
        scores[n] = per_head

    def value_products(n):
        r = chunks[n]
        intra[n] = [jnp.dot(scores[n][h], v_all[r, sl], preferred_element_type=F32)
                    for h, sl in enumerate(heads)]
        kv[n] = [lax.dot_general(v_all[r, sl], k_state[n][:, sl], TN_DIMS,
                                 preferred_element_type=F32) for sl in heads]

    phases = (exponents, decayed_operands, level_products, assemble_scores, value_products)
    for k in range(HG_STEP_CHUNKS + len(phases) - 1):
        for lag, phase in enumerate(phases):
            if 0 <= k - lag < HG_STEP_CHUNKS:
                phase(k - lag)

    for n, r in enumerate(chunks):
        outs = []
        for h, sl in enumerate(heads):
            st = state_ref[h]
            o_h = intra[n][h] + lax.dot_general(q_inter[n][:, sl], st.astype(BF16), NT_DIMS,
                                                preferred_element_type=F32)
            state_ref[h] = st * state_decay[n][:, sl] + kv[n][h]
            outs.append(o_h * lax.rsqrt(jnp.mean(o_h * o_h, axis=-1, keepdims=True) + RMS_EPS))
        o = jnp.concatenate(outs, axis=1) * ng_ref[...]
        half_gate = 0.5 * gate_ref[0, r, :]
        o_ref[0, r, :] = (o * (half_gate + half_gate * jnp.tanh(half_gate))).astype(o_ref.dtype)


def _hgrn(hq, hf, hi, hgate, lb_logits, hg_norm_g, seq):
    batch = hq.shape[0]
    rows = HG_STEP_CHUNKS * HG_CHUNK
    assert FRONT == rows, "the first grid step must hold exactly the rows in front of x"
    in_spec = pl.BlockSpec((1, rows, HG_WIDTH), lambda b, j: (b, j, 0))
    zmat = _hgrn_exponent_matrix()
    masks = _hgrn_level_masks()
    return pl.pallas_call(
        _hgrn_kernel,
        grid=(batch, (FRONT + seq) // rows),
        in_specs=[in_spec, in_spec, in_spec, in_spec,
                  _const_spec(lb_logits.shape), _const_spec((1, HG_WIDTH)),
                  _const_spec(zmat.shape), _const_spec(masks.shape)],
        out_specs=pl.BlockSpec((1, rows, HG_WIDTH), lambda b, j: (b, jnp.maximum(j - 1, 0), 0)),
        out_shape=jax.ShapeDtypeStruct((batch, seq, HG_WIDTH), BF16),
        scratch_shapes=[pltpu.VMEM((HG_HEADS, HG_DIM, HG_DIM), F32)],
        compiler_params=pltpu.CompilerParams(
            dimension_semantics=("parallel", "arbitrary"), vmem_limit_bytes=VMEM_LIMIT),
        name="hgrn2",
    )(hq, hf, hi, hgate, lb_logits, hg_norm_g, zmat, masks)


def _ffn_kernel(x_ref, osb_ref, ohg_ref, wout_ref, g2_ref, wg_ref, wu_ref, wd_ref, gf_ref, o_ref):
    half = FFN_ROWS // 2
    width = D_FF // FFN_SPLIT
    h1, u2, ffn = {}, {}, {}

    def mix_and_norm(r):
        rows = slice(r * half, (r + 1) * half)
        mix = (jnp.dot(osb_ref[0, rows, :], wout_ref[0:SB_WIDTH, :], preferred_element_type=F32)
               + jnp.dot(ohg_ref[0, rows, :], wout_ref[SB_WIDTH:, :], preferred_element_type=F32))
        h1[r] = x_ref[0, rows, :] + mix
        u2[r] = (h1[r] * lax.rsqrt(jnp.mean(h1[r] * h1[r], axis=-1, keepdims=True) + RMS_EPS)
                 * g2_ref[...]).astype(BF16)

    def swiglu(r):
        acc = jnp.zeros_like(h1[r])
        for n in range(FFN_SPLIT):
            cols = slice(n * width, (n + 1) * width)
            gate = jnp.dot(u2[r], wg_ref[:, cols], preferred_element_type=F32)
            up = jnp.dot(u2[r], wu_ref[:, cols], preferred_element_type=F32)
            half_gate = 0.5 * gate
            act = ((half_gate + half_gate * jnp.tanh(half_gate)) * up).astype(BF16)
            acc = acc + jnp.dot(act, wd_ref[cols, :], preferred_element_type=F32)
        ffn[r] = acc

    def final_norm(r):
        rows = slice(r * half, (r + 1) * half)
        h2 = h1[r] + ffn[r]
        o_ref[0, rows, :] = (h2 * lax.rsqrt(jnp.mean(h2 * h2, axis=-1, keepdims=True) + RMS_EPS)
                             * gf_ref[...])

    phases = (mix_and_norm, swiglu, final_norm)
    for k in range(2 + len(phases) - 1):
        for lag, phase in enumerate(phases):
            if 0 <= k - lag < 2:
                phase(k - lag)


def _out_projection_ffn(x, o_sb, o_hg, w_out, norm2_g, w_gate, w_up, w_down, final_g):
    batch, seq, _ = x.shape
    row_spec = lambda width: pl.BlockSpec((1, FFN_ROWS, width), lambda b, j: (b, j, 0))
    resident = lambda shape: pl.BlockSpec(shape, lambda b, j: (0, 0), pipeline_mode=pl.Buffered(1))
    return pl.pallas_call(
        _ffn_kernel,
        grid=(batch, seq // FFN_ROWS),
        in_specs=[row_spec(D_MODEL), row_spec(SB_WIDTH), row_spec(HG_WIDTH),
                  resident((D_MODEL, D_MODEL)), resident((1, D_MODEL)),
                  resident((D_MODEL, D_FF)), resident((D_MODEL, D_FF)), resident((D_FF, D_MODEL)),
                  resident((1, D_MODEL))],
        out_specs=row_spec(D_MODEL),
        out_shape=jax.ShapeDtypeStruct((batch, seq, D_MODEL), x.dtype),
        compiler_params=pltpu.CompilerParams(
            dimension_semantics=("parallel", "parallel"), vmem_limit_bytes=VMEM_LIMIT),
        name="out_projection_ffn",
    )(x, o_sb, o_hg, w_out, norm2_g, w_gate, w_up, w_down, final_g)


def kernel(x, meta_tokens, norm1_g, w_in, sb_norm_g, hg_norm_g, hg_lb_logits, w_out, norm2_g,
           w_gate, w_up, w_down, final_norm_g):
    batch, seq, d_model = x.shape
    assert d_model == D_MODEL and seq % FFN_ROWS == 0 and seq % ATT_TILE == 0
    assert norm1_g.shape[0] == 1, "single-layer block"
    head = jnp.concatenate(
        [jnp.zeros((FIRST_VALID, D_MODEL), x.dtype), meta_tokens.astype(x.dtype)], axis=0)
    row = lambda g: g.reshape(1, -1).astype(F32)

    qkv, hq, hf, hi, hgate = _input_projection(x, head, row(norm1_g[0]), w_in[0].astype(F32))
    o_sb = _attention(qkv, row(sb_norm_g[0]), seq)
    o_hg = _hgrn(hq, hf, hi, hgate, hg_lb_logits.astype(F32), row(hg_norm_g[0]), seq)
    return _out_projection_ffn(
        x, o_sb, o_hg, w_out[0].astype(BF16), row(norm2_g[0]), w_gate[0].astype(BF16),
        w_up[0].astype(BF16), w_down[0].astype(BF16), row(final_norm_g))
```

```python
import functools

import jax
import jax.numpy as jnp
import numpy as np
from jax import lax
from jax.experimental import pallas as pl
from jax.experimental.pallas import tpu as pltpu

F32 = jnp.float32
BF16 = jnp.bfloat16

D_MODEL = 1024
N_META = 16
SB_HEADS = 8
SB_HEAD_DIM = 64
SB_WIDTH = SB_HEADS * SB_HEAD_DIM
HG_HEADS = 4
HG_DIM = 128
HG_WIDTH = HG_HEADS * HG_DIM
IN_WIDTH = 3 * SB_WIDTH + 4 * HG_WIDTH
D_FF = 2816
RMS_EPS = 1e-6

LANES = 128
SUBLANES = 8
FRONT = 256
FIRST_VALID = FRONT - N_META
ATT_TILE = 256
ATT_PAIRS = 4
HG_CHUNK = 128
HG_LEVELS = 7
HG_STEP_CHUNKS = 2
PROJ_ROWS = 256
FFN_ROWS = 1024
FFN_SPLIT = 1
VMEM_LIMIT = 56 * 1024 * 1024
Q_SCALE = float(np.log2(np.e)) * SB_HEAD_DIM ** -0.5
MASKED = -1e9
SP_LINEAR = 64.0
EXIT_SUM = 160.0

NT_DIMS = (((1,), (1,)), ((), ()))
TN_DIMS = (((0,), (0,)), ((), ()))


def _const_spec(shape):
    return pl.BlockSpec(shape, lambda *_: (0,) * len(shape))


def _inproj_kernel(x_ref, head_ref, g_ref, w32_ref, qkv_ref, hq_ref, hf_ref, hi_ref, hg_ref,
                   w_ref, u_ref):
    j = pl.program_id(1)

    def normed(h):
        u = h * lax.rsqrt(jnp.mean(h * h, axis=-1, keepdims=True) + RMS_EPS) * g_ref[...]
        return u.astype(BF16)

    @pl.when(j == 0)
    def _():
        w_ref[...] = w32_ref[...].astype(BF16)
        u_ref[0] = normed(head_ref[...])

    ub = u_ref[j % 2]

    def proj(lo, width):
        return jnp.dot(ub, w_ref[:, lo:lo + width], preferred_element_type=F32)

    qkv_ref[0, :, 0:SB_WIDTH] = (proj(0, SB_WIDTH) * Q_SCALE).astype(BF16)
    qkv_ref[0, :, SB_WIDTH:3 * SB_WIDTH] = proj(SB_WIDTH, 2 * SB_WIDTH).astype(BF16)
    base = 3 * SB_WIDTH
    hq_ref[0] = proj(base, HG_WIDTH).astype(BF16)
    hf_ref[0] = proj(base + HG_WIDTH, HG_WIDTH)
    hi_ref[0] = proj(base + 2 * HG_WIDTH, HG_WIDTH).astype(BF16)
    hg_ref[0] = proj(base + 3 * HG_WIDTH, HG_WIDTH)
    u_ref[(j + 1) % 2] = normed(x_ref[0])


def _input_projection(x, head, norm_g, w_in):
    batch, seq, _ = x.shape
    n_tiles = (FRONT + seq) // PROJ_ROWS
    rows = FRONT + seq
    row_spec = lambda width: pl.BlockSpec((1, PROJ_ROWS, width), lambda b, j: (b, j, 0))
    return pl.pallas_call(
        _inproj_kernel,
        grid=(batch, n_tiles),
        in_specs=[
            pl.BlockSpec((1, PROJ_ROWS, D_MODEL), lambda b, j: (b, jnp.minimum(j, n_tiles - 2), 0)),
            _const_spec((PROJ_ROWS, D_MODEL)),
            _const_spec((1, D_MODEL)),
            pl.BlockSpec((D_MODEL, IN_WIDTH), lambda b, j: (0, 0), pipeline_mode=pl.Buffered(1)),
        ],
        out_specs=[row_spec(3 * SB_WIDTH), row_spec(HG_WIDTH), row_spec(HG_WIDTH),
                   row_spec(HG_WIDTH), row_spec(HG_WIDTH)],
        out_shape=[
            jax.ShapeDtypeStruct((batch, rows, 3 * SB_WIDTH), BF16),
            jax.ShapeDtypeStruct((batch, rows, HG_WIDTH), BF16),
            jax.ShapeDtypeStruct((batch, rows, HG_WIDTH), F32),
            jax.ShapeDtypeStruct((batch, rows, HG_WIDTH), BF16),
            jax.ShapeDtypeStruct((batch, rows, HG_WIDTH), F32),
        ],
        scratch_shapes=[pltpu.VMEM((D_MODEL, IN_WIDTH), BF16),
                        pltpu.VMEM((2, PROJ_ROWS, D_MODEL), BF16)],
        compiler_params=pltpu.CompilerParams(
            dimension_semantics=("arbitrary", "arbitrary"), vmem_limit_bytes=VMEM_LIMIT),
        name="in_projection",
    )(x, head, norm_g, w_in)


def _attn_kernel(q_ref, k_ref, v_ref, tri_ref, g_ref, o_ref, qh_ref, z_ref, c_ref, sum_ref, acc_ref):
    tile = ATT_TILE
    n_heads = qh_ref.shape[0]
    i = pl.program_id(2) + FRONT // ATT_TILE
    last = i - FIRST_VALID // ATT_TILE
    lane = lax.broadcasted_iota(jnp.int32, (1, LANES), 1)
    first_head = lane < SB_HEAD_DIM
    t_pos = i * tile + lax.broadcasted_iota(jnp.int32, (1, tile), 1)

    def pair_lanes(h):
        return slice((h // 2) * LANES, (h // 2 + 1) * LANES)

    def rows_of(c):
        return pl.ds(pl.multiple_of(c * tile, tile), tile)

    def minus_row(a, row8):
        return (a.reshape(tile // SUBLANES, SUBLANES, tile) - row8[None]).reshape(tile, tile)

    def process(positions, from_diagonal=False):
        half = tile // 2

        def on_visible_quadrants(fn, *arrays):
            top = fn(*[a[0:half, :] for a in arrays])
            corner = fn(*[a[half:, half:] for a in arrays])
            return jnp.concatenate(
                [top, jnp.concatenate([jnp.zeros_like(corner), corner], axis=1)], axis=0)

        def softplus_bf16(z):
            zb = z.astype(BF16)
            grown = 1.0 + jnp.exp2(jnp.minimum(zb, SP_LINEAR))
            return jnp.maximum(zb, jnp.log2(grown.astype(F32)).astype(BF16))

        def weight_bf16(y, local):
            return jnp.exp2((y - local).astype(BF16))
        visible = []
        for n in positions:
            s_pos = (i - n) * tile + lax.broadcasted_iota(jnp.int32, (tile, 1), 0)
            visible.append((s_pos < t_pos) & (s_pos >= FIRST_VALID))

        def scores(h):
            for slot, n in enumerate(positions):
                z = lax.dot_general(k_ref[0, rows_of(i - n), pair_lanes(h)], qh_ref[h], NT_DIMS,
                                    preferred_element_type=F32)
                z_ref[slot, h] = jnp.where(visible[slot], z, MASKED)

        def suffix_sums(h):
            for slot in range(len(positions)):
                if from_diagonal and slot == 0:
                    sp = on_visible_quadrants(softplus_bf16, z_ref[slot, h])
                else:
                    sp = softplus_bf16(z_ref[slot, h])
                c_ref[slot, h] = jnp.dot(tri_ref[...], sp, preferred_element_type=F32)

        def weights(h):
            later = sum_ref[h]
            acc = acc_ref[h]
            for slot, n in enumerate(positions):
                local = c_ref[slot, h]
                if from_diagonal and slot == 0:
                    w = on_visible_quadrants(weight_bf16, z_ref[slot, h], local)
                else:
                    w = weight_bf16(minus_row(z_ref[slot, h], later), local)
                acc = acc + lax.dot_general(v_ref[0, rows_of(i - n), pair_lanes(h)], w, TN_DIMS,
                                            preferred_element_type=F32)
                later = later + jnp.broadcast_to(local[0:1, :], (SUBLANES, tile))
            sum_ref[h] = later
            acc_ref[h] = acc

        phases = (scores, suffix_sums, weights)
        if len(positions) == 1:
            for phase in phases:
                for h in range(n_heads):
                    phase(h)
            return
        for k in range(n_heads + len(phases) - 1):
            for lag, phase in enumerate(phases):
                if 0 <= k - lag < n_heads:
                    phase(k - lag)

    for h in range(n_heads):
        q_pair = q_ref[0, :, pair_lanes(h)]
        qh_ref[h] = jnp.where(first_head == (h % 2 == 0), q_pair, jnp.zeros_like(q_pair))
    sum_ref[...] = jnp.zeros(sum_ref.shape, F32)
    acc_ref[...] = jnp.zeros(acc_ref.shape, F32)
    process([0, 1], from_diagonal=True)

    def sweep(carry):
        n, _ = carry
        process([n])
        return n + 1, jnp.min(sum_ref[...])

    lax.while_loop(lambda carry: (carry[0] <= last) & (carry[1] < EXIT_SUM),
                   sweep, (jnp.int32(2), jnp.min(sum_ref[...])))

    for p in range(n_heads // 2):
        o = jnp.concatenate([acc_ref[2 * p, 0:SB_HEAD_DIM], acc_ref[2 * p + 1, SB_HEAD_DIM:LANES]],
                            axis=0).T
        sq = o * o
        ss_first = jnp.sum(jnp.where(first_head, sq, 0.0), axis=-1, keepdims=True)
        ss_second = jnp.sum(sq, axis=-1, keepdims=True) - ss_first
        mean_sq = jnp.where(first_head, ss_first, ss_second) * (1.0 / SB_HEAD_DIM)
        lanes = slice(p * LANES, (p + 1) * LANES)
        o_ref[0, :, lanes] = (o * lax.rsqrt(mean_sq + RMS_EPS) * g_ref[:, lanes]).astype(o_ref.dtype)


def _reverse_cumsum_matrix(n):
    j = np.arange(n)[:, None]
    s = np.arange(n)[None, :]
    return jnp.asarray((s >= j).astype(np.float32), BF16)


def _attention(qkv, sb_norm_g, seq):
    batch, rows, _ = qkv.shape
    width = ATT_PAIRS * LANES
    groups = SB_WIDTH // width
    n_q = seq // ATT_TILE
    heads = 2 * ATT_PAIRS
    return pl.pallas_call(
        _attn_kernel,
        grid=(batch, groups, n_q),
        in_specs=[
            pl.BlockSpec((1, ATT_TILE, width), lambda b, g, i: (b, i + FRONT // ATT_TILE, g)),
            pl.BlockSpec((1, rows, width), lambda b, g, i: (b, 0, groups + g)),
            pl.BlockSpec((1, rows, width), lambda b, g, i: (b, 0, 2 * groups + g)),
            _const_spec((ATT_TILE, ATT_TILE)),
            pl.BlockSpec((1, width), lambda b, g, i: (0, g)),
        ],
        out_specs=pl.BlockSpec((1, ATT_TILE, width), lambda b, g, i: (b, i, g)),
        out_shape=jax.ShapeDtypeStruct((batch, seq, SB_WIDTH), BF16),
        scratch_shapes=[pltpu.VMEM((heads, ATT_TILE, LANES), BF16),
                        pltpu.VMEM((2, heads, ATT_TILE, ATT_TILE), F32),
                        pltpu.VMEM((2, heads, ATT_TILE, ATT_TILE), F32),
                        pltpu.VMEM((heads, SUBLANES, ATT_TILE), F32),
                        pltpu.VMEM((heads, LANES, ATT_TILE), F32)],
        compiler_params=pltpu.CompilerParams(
            dimension_semantics=("parallel", "parallel", "parallel"), vmem_limit_bytes=VMEM_LIMIT),
        name="stickbreak_attention",
    )(qkv, qkv, qkv, _reverse_cumsum_matrix(ATT_TILE), sb_norm_g)


def _hgrn_exponent_matrix():
    c = HG_CHUNK
    t = np.arange(c)[:, None]
    j = np.arange(c)[None, :]
    mats = [(j <= t)]
    for lvl in range(HG_LEVELS):
        m = c >> (lvl + 1)
        blk = t // m
        odd = (blk % 2) == 1
        prefix = (j >= blk * m) & (j <= t)
        suffix = (j > t) & (j <= blk * m + m - 1)
        mats.append(np.where(odd, prefix, suffix))
    z = np.concatenate(mats, axis=0).astype(np.float32)
    return jnp.asarray(np.concatenate([z, z], axis=1), BF16)


def _hgrn_level_masks():
    c = HG_CHUNK
    t = np.arange(c)[:, None]
    s = np.arange(c)[None, :]
    masks = [t == s]
    for lvl in range(HG_LEVELS):
        m = c >> (lvl + 1)
        masks.append(((t // m) == (s // m) + 1) & (((s // m) % 2) == 0))
    return jnp.asarray(np.stack(masks).astype(np.float32))


def _hgrn_kernel(q_ref, f_ref, i_ref, gate_ref, lbl_ref, ng_ref, zmat_ref, mask_ref, o_ref,
                 state_ref):
    j = pl.program_id(1)
    ch = HG_CHUNK
    rows = HG_STEP_CHUNKS * ch

    @pl.when(j == 0)
    def _():
        state_ref[...] = jnp.zeros_like(state_ref)

    logits = lbl_ref[...]
    mx = jnp.max(logits, axis=0, keepdims=True)
    ex = jnp.exp(logits - mx)
    lb = ex[0:1] / jnp.sum(ex, axis=0, keepdims=True)

    f_raw = f_ref[0]
    sig = 0.5 + 0.5 * jnp.tanh(0.5 * f_raw)
    log_f = jnp.log2(lb + (1.0 - lb) * sig)
    kk = (1.0 - lb) * (1.0 - sig)
    row_all = lax.broadcasted_iota(jnp.int32, (rows, 1), 0)
    valid = (j > 0) | (row_all >= FIRST_VALID)
    log_f = jnp.where(valid, log_f, 0.0)
    kk = jnp.where(valid, kk, 0.0)
    hi = log_f.astype(BF16)
    lo = (log_f - hi.astype(F32)).astype(BF16)
    q_all = q_ref[0].astype(F32)
    v_all = i_ref[0]

    chunks = [slice(n * ch, (n + 1) * ch) for n in range(HG_STEP_CHUNKS)]
    heads = [slice(h * HG_DIM, (h + 1) * HG_DIM) for h in range(HG_HEADS)]
    row = lax.broadcasted_iota(jnp.int32, (ch, 1), 0)
    expo, sides, q_bf, k_bf, q_inter, k_state, state_decay = {}, {}, {}, {}, {}, {}, {}
    parts, scores, intra, kv = {}, {}, {}, {}

    def exponents(n):
        r = chunks[n]
        expo[n] = jnp.dot(zmat_ref[...], jnp.concatenate([hi[r], lo[r]], axis=0),
                          preferred_element_type=F32)

    def decayed_operands(n):
        q, k = q_all[chunks[n]], kk[chunks[n]]
        level_sides = []
        for lvl in range(HG_LEVELS):
            m = ch >> (lvl + 1)
            decay = jnp.exp2(expo[n][(lvl + 1) * ch:(lvl + 2) * ch])
            if m % SUBLANES == 0:
                both = jnp.concatenate([(q if blk % 2 else k)[blk * m:(blk + 1) * m]
                                        for blk in range(ch // m)], axis=0)
            else:
                both = jnp.where(((row // m) % 2) == 1, q, k)
            level_sides.append((both * decay).astype(BF16))
        sides[n] = level_sides
        q_bf[n] = q.astype(BF16)
        k_bf[n] = k.astype(BF16)
        b = expo[n][0:ch]
        b_last = b[ch - 1:ch]
        q_inter[n] = (q * jnp.exp2(b)).astype(BF16)
        k_state[n] = (k * jnp.exp2(b_last - b)).astype(BF16)
        state_decay[n] = jnp.exp2(b_last)

    def level_products(n):
        parts[n] = [[lax.dot_general(x[:, sl], y[:, sl], NT_DIMS, preferred_element_type=F32)
                     for x, y in [(q_bf[n], k_bf[n])] + [(sd, sd) for sd in sides[n]]]
                    for sl in heads]

    def assemble_scores(n):
        per_head = []
        for h in range(HG_HEADS):
            sc = parts[n][h][0] * mask_ref[0]
            for lvl in range(HG_LEVELS):
                sc = sc + parts[n][h][lvl + 1] * mask_ref[lvl + 1]
            per_head.append(sc.astype(BF16))
        scores[n] = per_head

    def value_products(n):
        r = chunks[n]
        intra[n] = [jnp.dot(scores[n][h], v_all[r, sl], preferred_element_type=F32)
                    for h, sl in enumerate(heads)]
        kv[n] = [lax.dot_general(v_all[r, sl], k_state[n][:, sl], TN_DIMS,
                                 preferred_element_type=F32) for sl in heads]

    phases = (exponents, decayed_operands, level_products, assemble_scores, value_products)
    for k in range(HG_STEP_CHUNKS + len(phases) - 1):
        for lag, phase in enumerate(phases):
            if 0 <= k - lag < HG_STEP_CHUNKS:
                phase(k - lag)

    for n, r in enumerate(chunks):
        outs = []
        for h, sl in enumerate(heads):
            st = state_ref[h]
            o_h = intra[n][h] + lax.dot_general(q_inter[n][:, sl], st.astype(BF16), NT_DIMS,
                                                preferred_element_type=F32)
            state_ref[h] = st * state_decay[n][:, sl] + kv[n][h]
            outs.append(o_h * lax.rsqrt(jnp.mean(o_h * o_h, axis=-1, keepdims=True) + RMS_EPS))
        o = jnp.concatenate(outs, axis=1) * ng_ref[...]
        half_gate = 0.5 * gate_ref[0, r, :]
        o_ref[0, r, :] = (o * (half_gate + half_gate * jnp.tanh(half_gate))).astype(o_ref.dtype)


def _hgrn(hq, hf, hi, hgate, lb_logits, hg_norm_g, seq):
    batch = hq.shape[0]
    rows = HG_STEP_CHUNKS * HG_CHUNK
    assert FRONT == rows, "the first grid step must hold exactly the rows in front of x"
    in_spec = pl.BlockSpec((1, rows, HG_WIDTH), lambda b, j: (b, j, 0))
    zmat = _hgrn_exponent_matrix()
    masks = _hgrn_level_masks()
    return pl.pallas_call(
        _hgrn_kernel,
        grid=(batch, (FRONT + seq) // rows),
        in_specs=[in_spec, in_spec, in_spec, in_spec,
                  _const_spec(lb_logits.shape), _const_spec((1, HG_WIDTH)),
                  _const_spec(zmat.shape), _const_spec(masks.shape)],
        out_specs=pl.BlockSpec((1, rows, HG_WIDTH), lambda b, j: (b, jnp.maximum(j - 1, 0), 0)),
        out_shape=jax.ShapeDtypeStruct((batch, seq, HG_WIDTH), BF16),
        scratch_shapes=[pltpu.VMEM((HG_HEADS, HG_DIM, HG_DIM), F32)],
        compiler_params=pltpu.CompilerParams(
            dimension_semantics=("parallel", "arbitrary"), vmem_limit_bytes=VMEM_LIMIT),
        name="hgrn2",
    )(hq, hf, hi, hgate, lb_logits, hg_norm_g, zmat, masks)


def _ffn_kernel(x_ref, osb_ref, ohg_ref, wout_ref, g2_ref, wg_ref, wu_ref, wd_ref, gf_ref, o_ref):
    half = FFN_ROWS // 2
    width = D_FF // FFN_SPLIT
    h1, u2, ffn = {}, {}, {}

    def mix_and_norm(r):
        rows = slice(r * half, (r + 1) * half)
        mix = (jnp.dot(osb_ref[0, rows, :], wout_ref[0:SB_WIDTH, :], preferred_element_type=F32)
               + jnp.dot(ohg_ref[0, rows, :], wout_ref[SB_WIDTH:, :], preferred_element_type=F32))
        h1[r] = x_ref[0, rows, :] + mix
        u2[r] = (h1[r] * lax.rsqrt(jnp.mean(h1[r] * h1[r], axis=-1, keepdims=True) + RMS_EPS)
                 * g2_ref[...]).astype(BF16)

    def swiglu(r):
        acc = jnp.zeros_like(h1[r])
        for n in range(FFN_SPLIT):
            cols = slice(n * width, (n + 1) * width)
            gate = jnp.dot(u2[r], wg_ref[:, cols], preferred_element_type=F32)
            up = jnp.dot(u2[r], wu_ref[:, cols], preferred_element_type=F32)
            half_gate = 0.5 * gate
            act = ((half_gate + half_gate * jnp.tanh(half_gate)) * up).astype(BF16)
            acc = acc + jnp.dot(act, wd_ref[cols, :], preferred_element_type=F32)
        ffn[r] = acc

    def final_norm(r):
        rows = slice(r * half, (r + 1) * half)
        h2 = h1[r] + ffn[r]
        o_ref[0, rows, :] = (h2 * lax.rsqrt(jnp.mean(h2 * h2, axis=-1, keepdims=True) + RMS_EPS)
                             * gf_ref[...])

    phases = (mix_and_norm, swiglu, final_norm)
    for k in range(2 + len(phases) - 1):
        for lag, phase in enumerate(phases):
            if 0 <= k - lag < 2:
                phase(k - lag)


def _out_projection_ffn(x, o_sb, o_hg, w_out, norm2_g, w_gate, w_up, w_down, final_g):
    batch, seq, _ = x.shape
    row_spec = lambda width: pl.BlockSpec((1, FFN_ROWS, width), lambda b, j: (b, j, 0))
    resident = lambda shape: pl.BlockSpec(shape, lambda b, j: (0, 0), pipeline_mode=pl.Buffered(1))
    return pl.pallas_call(
        _ffn_kernel,
        grid=(batch, seq // FFN_ROWS),
        in_specs=[row_spec(D_MODEL), row_spec(SB_WIDTH), row_spec(HG_WIDTH),
                  resident((D_MODEL, D_MODEL)), resident((1, D_MODEL)),
                  resident((D_MODEL, D_FF)), resident((D_MODEL, D_FF)), resident((D_FF, D_MODEL)),
                  resident((1, D_MODEL))],
        out_specs=row_spec(D_MODEL),
        out_shape=jax.ShapeDtypeStruct((batch, seq, D_MODEL), x.dtype),
        compiler_params=pltpu.CompilerParams(
            dimension_semantics=("parallel", "parallel"), vmem_limit_bytes=VMEM_LIMIT),
        name="out_projection_ffn",
    )(x, o_sb, o_hg, w_out, norm2_g, w_gate, w_up, w_down, final_g)


def kernel(x, meta_tokens, norm1_g, w_in, sb_norm_g, hg_norm_g, hg_lb_logits, w_out, norm2_g,
           w_gate, w_up, w_down, final_norm_g):
    batch, seq, d_model = x.shape
    assert d_model == D_MODEL and seq % FFN_ROWS == 0 and seq % ATT_TILE == 0
    assert norm1_g.shape[0] == 1, "single-layer block"
    head = jnp.concatenate(
        [jnp.zeros((FIRST_VALID, D_MODEL), x.dtype), meta_tokens.astype(x.dtype)], axis=0)
    row = lambda g: g.reshape(1, -1).astype(F32)

    qkv, hq, hf, hi, hgate = _input_projection(x, head, row(norm1_g[0]), w_in[0].astype(F32))
    o_sb = _attention(qkv, row(sb_norm_g[0]), seq)
    o_hg = _hgrn(hq, hf, hi, hgate, hg_lb_logits.astype(F32), row(hg_norm_g[0]), seq)
    return _out_projection_ffn(
        x, o_sb, o_hg, w_out[0].astype(BF16), row(norm2_g[0]), w_gate[0].astype(BF16),
        w_up[0].astype(BF16), w_down[0].astype(BF16), row(final_norm_g))
```

```python
import functools

import jax
import jax.numpy as jnp
import numpy as np
from jax import lax
from jax.experimental import pallas as pl
from jax.experimental.pallas import tpu as pltpu

F32 = jnp.float32
BF16 = jnp.bfloat16

D_MODEL = 1024
N_META = 16
SB_HEADS = 8
SB_HEAD_DIM = 64
SB_WIDTH = SB_HEADS * SB_HEAD_DIM
HG_HEADS = 4
HG_DIM = 128
HG_WIDTH = HG_HEADS * HG_DIM
IN_WIDTH = 3 * SB_WIDTH + 4 * HG_WIDTH
D_FF = 2816
RMS_EPS = 1e-6

LANES = 128
SUBLANES = 8
FRONT = 256
FIRST_VALID = FRONT - N_META
ATT_TILE = 256
ATT_PAIRS = 4
HG_CHUNK = 128
HG_LEVELS = 7
HG_STEP_CHUNKS = 2
PROJ_ROWS = 256
FFN_ROWS = 1024
FFN_SPLIT = 1
VMEM_LIMIT = 56 * 1024 * 1024
Q_SCALE = float(np.log2(np.e)) * SB_HEAD_DIM ** -0.5
MASKED = -1e9
SP_LINEAR = 64.0
EXIT_SUM = 160.0

NT_DIMS = (((1,), (1,)), ((), ()))
TN_DIMS = (((0,), (0,)), ((), ()))


def _const_spec(shape):
    return pl.BlockSpec(shape, lambda *_: (0,) * len(shape))


def _inproj_kernel(x_ref, head_ref, g_ref, w32_ref, qkv_ref, hq_ref, hf_ref, hi_ref, hg_ref,
                   w_ref, u_ref):
    j = pl.program_id(1)

    def normed(h):
        u = h * lax.rsqrt(jnp.mean(h * h, axis=-1, keepdims=True) + RMS_EPS) * g_ref[...]
        return u.astype(BF16)

    @pl.when(j == 0)
    def _():
        w_ref[...] = w32_ref[...].astype(BF16)
        u_ref[0] = normed(head_ref[...])

    ub = u_ref[j % 2]

    def proj(lo, width):
        return jnp.dot(ub, w_ref[:, lo:lo + width], preferred_element_type=F32)

    qkv_ref[0, :, 0:SB_WIDTH] = (proj(0, SB_WIDTH) * Q_SCALE).astype(BF16)
    qkv_ref[0, :, SB_WIDTH:3 * SB_WIDTH] = proj(SB_WIDTH, 2 * SB_WIDTH).astype(BF16)
    base = 3 * SB_WIDTH
    hq_ref[0] = proj(base, HG_WIDTH).astype(BF16)
    hf_ref[0] = proj(base + HG_WIDTH, HG_WIDTH)
    hi_ref[0] = proj(base + 2 * HG_WIDTH, HG_WIDTH).astype(BF16)
    hg_ref[0] = proj(base + 3 * HG_WIDTH, HG_WIDTH)
    u_ref[(j + 1) % 2] = normed(x_ref[0])


def _input_projection(x, head, norm_g, w_in):
    batch, seq, _ = x.shape
    n_tiles = (FRONT + seq) // PROJ_ROWS
    rows = FRONT + seq
    row_spec = lambda width: pl.BlockSpec((1, PROJ_ROWS, width), lambda b, j: (b, j, 0))
    return pl.pallas_call(
        _inproj_kernel,
        grid=(batch, n_tiles),
        in_specs=[
            pl.BlockSpec((1, PROJ_ROWS, D_MODEL), lambda b, j: (b, jnp.minimum(j, n_tiles - 2), 0)),
            _const_spec((PROJ_ROWS, D_MODEL)),
            _const_spec((1, D_MODEL)),
            pl.BlockSpec((D_MODEL, IN_WIDTH), lambda b, j: (0, 0), pipeline_mode=pl.Buffered(1)),
        ],
        out_specs=[row_spec(3 * SB_WIDTH), row_spec(HG_WIDTH), row_spec(HG_WIDTH),
                   row_spec(HG_WIDTH), row_spec(HG_WIDTH)],
        out_shape=[
            jax.ShapeDtypeStruct((batch, rows, 3 * SB_WIDTH), BF16),
            jax.ShapeDtypeStruct((batch, rows, HG_WIDTH), BF16),
            jax.ShapeDtypeStruct((batch, rows, HG_WIDTH), F32),
            jax.ShapeDtypeStruct((batch, rows, HG_WIDTH), BF16),
            jax.ShapeDtypeStruct((batch, rows, HG_WIDTH), F32),
        ],
        scratch_shapes=[pltpu.VMEM((D_MODEL, IN_WIDTH), BF16),
                        pltpu.VMEM((2, PROJ_ROWS, D_MODEL), BF16)],
        compiler_params=pltpu.CompilerParams(
            dimension_semantics=("arbitrary", "arbitrary"), vmem_limit_bytes=VMEM_LIMIT),
        name="in_projection",
    )(x, head, norm_g, w_in)


def _attn_kernel(q_ref, k_ref, v_ref, tri_ref, g_ref, o_ref, qh_ref, z_ref, c_ref, sum_ref, acc_ref):
    tile = ATT_TILE
    n_heads = qh_ref.shape[0]
    i = pl.program_id(2) + FRONT // ATT_TILE
    last = i - FIRST_VALID // ATT_TILE
    lane = lax.broadcasted_iota(jnp.int32, (1, LANES), 1)
    first_head = lane < SB_HEAD_DIM
    t_pos = i * tile + lax.broadcasted_iota(jnp.int32, (1, tile), 1)

    def pair_lanes(h):
        return slice((h // 2) * LANES, (h // 2 + 1) * LANES)

    def rows_of(c):
        return pl.ds(pl.multiple_of(c * tile, tile), tile)

    def minus_row(a, row8):
        return (a.reshape(tile // SUBLANES, SUBLANES, tile) - row8[None]).reshape(tile, tile)

    def process(positions, from_diagonal=False):
        half = tile // 2

        def on_visible_quadrants(fn, *arrays):
            top = fn(*[a[0:half, :] for a in arrays])
            corner = fn(*[a[half:, half:] for a in arrays])
            return jnp.concatenate(
                [top, jnp.concatenate([jnp.zeros_like(corner), corner], axis=1)], axis=0)

        def softplus_bf16(z):
            zb = z.astype(BF16)
            grown = 1.0 + jnp.exp2(jnp.minimum(zb, SP_LINEAR))
            return jnp.maximum(zb, jnp.log2(grown.astype(F32)).astype(BF16))

        def weight_bf16(y, local):
            return jnp.exp2((y - local).astype(BF16))
        visible = []
        for n in positions:
            s_pos = (i - n) * tile + lax.broadcasted_iota(jnp.int32, (tile, 1), 0)
            visible.append((s_pos < t_pos) & (s_pos >= FIRST_VALID))

        def scores(h):
            for slot, n in enumerate(positions):
                z = lax.dot_general(k_ref[0, rows_of(i - n), pair_lanes(h)], qh_ref[h], NT_DIMS,
                                    preferred_element_type=F32)
                z_ref[slot, h] = jnp.where(visible[slot], z, MASKED)

        def suffix_sums(h):
            for slot in range(len(positions)):
                if from_diagonal and slot == 0:
                    sp = on_visible_quadrants(softplus_bf16, z_ref[slot, h])
                else:
                    sp = softplus_bf16(z_ref[slot, h])
                c_ref[slot, h] = jnp.dot(tri_ref[...], sp, preferred_element_type=F32)

        def weights(h):
            later = sum_ref[h]
            acc = acc_ref[h]
            for slot, n in enumerate(positions):
                local = c_ref[slot, h]
                if from_diagonal and slot == 0:
                    w = on_visible_quadrants(weight_bf16, z_ref[slot, h], local)
                else:
                    w = weight_bf16(minus_row(z_ref[slot, h], later), local)
                acc = acc + lax.dot_general(v_ref[0, rows_of(i - n), pair_lanes(h)], w, TN_DIMS,
                                            preferred_element_type=F32)
                later = later + jnp.broadcast_to(local[0:1, :], (SUBLANES, tile))
            sum_ref[h] = later
            acc_ref[h] = acc

        phases = (scores, suffix_sums, weights)
        if len(positions) == 1:
            for phase in phases:
                for h in range(n_heads):
                    phase(h)
            return
        for k in range(n_heads + len(phases) - 1):
            for lag, phase in enumerate(phases):
                if 0 <= k - lag < n_heads:
                    phase(k - lag)

    for h in range(n_heads):
        q_pair = q_ref[0, :, pair_lanes(h)]
        qh_ref[h] = jnp.where(first_head == (h % 2 == 0), q_pair, jnp.zeros_like(q_pair))
    sum_ref[...] = jnp.zeros(sum_ref.shape, F32)
    acc_ref[...] = jnp.zeros(acc_ref.shape, F32)
    process([0, 1], from_diagonal=True)

    def sweep(carry):
        n, _ = carry
        process([n])
        return n + 1, jnp.min(sum_ref[...])

    lax.while_loop(lambda carry: (carry[0] <= last) & (carry[1] < EXIT_SUM),
                   sweep, (jnp.int32(2), jnp.min(sum_ref[...])))

    for p in range(n_heads // 2):
        o = jnp.concatenate([acc_ref[2 * p, 0:SB_HEAD_DIM], acc_ref[2 * p + 1, SB_HEAD_DIM:LANES]],
                            axis=0).T
        sq = o * o
        ss_first = jnp.sum(jnp.where(first_head, sq, 0.0), axis=-1, keepdims=True)
        ss_second = jnp.sum(sq, axis=-1, keepdims=True) - ss_first
        mean_sq = jnp.where(first_head, ss_first, ss_second) * (1.0 / SB_HEAD_DIM)
        lanes = slice(p * LANES, (p + 1) * LANES)
        o_ref[0, :, lanes] = (o * lax.rsqrt(mean_sq + RMS_EPS) * g_ref[:, lanes]).astype(o_ref.dtype)


def _reverse_cumsum_matrix(n):
    j = np.arange(n)[:, None]
    s = np.arange(n)[None, :]
    return jnp.asarray((s >= j).astype(np.float32), BF16)


def _attention(qkv, sb_norm_g, seq):
    batch, rows, _ = qkv.shape
    width = ATT_PAIRS * LANES
    groups = SB_WIDTH // width
    n_q = seq // ATT_TILE
    heads = 2 * ATT_PAIRS
    return pl.pallas_call(
        _attn_kernel,
        grid=(batch, groups, n_q),
        in_specs=[
            pl.BlockSpec((1, ATT_TILE, width), lambda b, g, i: (b, i + FRONT // ATT_TILE, g)),
            pl.BlockSpec((1, rows, width), lambda b, g, i: (b, 0, groups + g)),
            pl.BlockSpec((1, rows, width), lambda b, g, i: (b, 0, 2 * groups + g)),
            _const_spec((ATT_TILE, ATT_TILE)),
            pl.BlockSpec((1, width), lambda b, g, i: (0, g)),
        ],
        out_specs=pl.BlockSpec((1, ATT_TILE, width), lambda b, g, i: (b, i, g)),
        out_shape=jax.ShapeDtypeStruct((batch, seq, SB_WIDTH), BF16),
        scratch_shapes=[pltpu.VMEM((heads, ATT_TILE, LANES), BF16),
                        pltpu.VMEM((2, heads, ATT_TILE, ATT_TILE), F32),
                        pltpu.VMEM((2, heads, ATT_TILE, ATT_TILE), F32),
                        pltpu.VMEM((heads, SUBLANES, ATT_TILE), F32),
                        pltpu.VMEM((heads, LANES, ATT_TILE), F32)],
        compiler_params=pltpu.CompilerParams(
            dimension_semantics=("parallel", "parallel", "parallel"), vmem_limit_bytes=VMEM_LIMIT),
        name="stickbreak_attention",
    )(qkv, qkv, qkv, _reverse_cumsum_matrix(ATT_TILE), sb_norm_g)


def _hgrn_exponent_matrix():
    c = HG_CHUNK
    t = np.arange(c)[:, None]
    j = np.arange(c)[None, :]
    mats = [(j <= t)]
    for lvl in range(HG_LEVELS):
        m = c >> (lvl + 1)
        blk = t // m
        odd = (blk % 2) == 1
        prefix = (j >= blk * m) & (j <= t)
        suffix = (j > t) & (j <= blk * m + m - 1)
        mats.append(np.where(odd, prefix, suffix))
    z = np.concatenate(mats, axis=0).astype(np.float32)
    return jnp.asarray(np.concatenate([z, z], axis=1), BF16)


def _hgrn_level_masks():
    c = HG_CHUNK
    t = np.arange(c)[:, None]
    s = np.arange(c)[None, :]
    masks = [t == s]
    for lvl in range(HG_LEVELS):
        m = c >> (lvl + 1)
        masks.append(((t // m) == (s // m) + 1) & (((s // m) % 2) == 0))
    return jnp.asarray(np.stack(masks).astype(np.float32))


def _hgrn_kernel(q_ref, f_ref, i_ref, gate_ref, lbl_ref, ng_ref, zmat_ref, mask_ref, o_ref,
                 state_ref):
    j = pl.program_id(1)
    ch = HG_CHUNK
    rows = HG_STEP_CHUNKS * ch

    @pl.when(j == 0)
    def _():
        state_ref[...] = jnp.zeros_like(state_ref)

    logits = lbl_ref[...]
    mx = jnp.max(logits, axis=0, keepdims=True)
    ex = jnp.exp(logits - mx)
    lb = ex[0:1] / jnp.sum(ex, axis=0, keepdims=True)

    f_raw = f_ref[0]
    sig = 0.5 + 0.5 * jnp.tanh(0.5 * f_raw)
    log_f = jnp.log2(lb + (1.0 - lb) * sig)
    kk = (1.0 - lb) * (1.0 - sig)
    row_all = lax.broadcasted_iota(jnp.int32, (rows, 1), 0)
    valid = (j > 0) | (row_all >= FIRST_VALID)
    log_f = jnp.where(valid, log_f, 0.0)
    kk = jnp.where(valid, kk, 0.0)
    hi = log_f.astype(BF16)
    lo = (log_f - hi.astype(F32)).astype(BF16)
    q_all = q_ref[0].astype(F32)
    v_all = i_ref[0]

    chunks = [slice(n * ch, (n + 1) * ch) for n in range(HG_STEP_CHUNKS)]
    heads = [slice(h * HG_DIM, (h + 1) * HG_DIM) for h in range(HG_HEADS)]
    row = lax.broadcasted_iota(jnp.int32, (ch, 1), 0)
    expo, sides, q_bf, k_bf, q_inter, k_state, state_decay = {}, {}, {}, {}, {}, {}, {}
    parts, scores, intra, kv = {}, {}, {}, {}

    def exponents(n):
        r = chunks[n]
        expo[n] = jnp.dot(zmat_ref[...], jnp.concatenate([hi[r], lo[r]], axis=0),
                          preferred_element_type=F32)

    def decayed_operands(n, h):
        sl = heads[h]
        q, k = q_all[chunks[n], sl], kk[chunks[n], sl]
        level_sides = []
        for lvl in range(HG_LEVELS):
            m = ch >> (lvl + 1)
            decay = jnp.exp2(expo[n][(lvl + 1) * ch:(lvl + 2) * ch, sl])
            if m % SUBLANES == 0:
                both = jnp.concatenate([(q if blk % 2 else k)[blk * m:(blk + 1) * m]
                                        for blk in range(ch // m)], axis=0)
            else:
                both = jnp.where(((row // m) % 2) == 1, q, k)
            level_sides.append((both * decay).astype(BF16))
        sides[n, h] = level_sides
        q_bf[n, h] = q.astype(BF16)
        k_bf[n, h] = k.astype(BF16)
        b = expo[n][0:ch, sl]
        b_last = b[ch - 1:ch]
        q_inter[n, h] = (q * jnp.exp2(b)).astype(BF16)
        k_state[n, h] = (k * jnp.exp2(b_last - b)).astype(BF16)
        state_decay[n, h] = jnp.exp2(b_last)

    def level_scores(n, h):
        pairs = [(q_bf[n, h], k_bf[n, h])] + [(sd, sd) for sd in sides[n, h]]
        prods = [lax.dot_general(x, y, NT_DIMS, preferred_element_type=F32) for x, y in pairs]
        sc = prods[0] * mask_ref[0]
        for lvl in range(HG_LEVELS):
            sc = sc + prods[lvl + 1] * mask_ref[lvl + 1]
        scores[n, h] = sc.astype(BF16)

    def value_products(n, h):
        r, sl = chunks[n], heads[h]
        intra[n, h] = jnp.dot(scores[n, h], v_all[r, sl], preferred_element_type=F32)
        kv[n, h] = lax.dot_general(v_all[r, sl], k_state[n, h], TN_DIMS,
                                   preferred_element_type=F32)

    for n in range(HG_STEP_CHUNKS):
        exponents(n)
    for n in range(HG_STEP_CHUNKS):
        for h in range(HG_HEADS):
            decayed_operands(n, h)
            level_scores(n, h)
            value_products(n, h)

    for n, r in enumerate(chunks):
        outs = []
        for h, sl in enumerate(heads):
            st = state_ref[h]
            o_h = intra[n, h] + lax.dot_general(q_inter[n, h], st.astype(BF16), NT_DIMS,
                                                preferred_element_type=F32)
            state_ref[h] = st * state_decay[n, h] + kv[n, h]
            outs.append(o_h * lax.rsqrt(jnp.mean(o_h * o_h, axis=-1, keepdims=True) + RMS_EPS))
        o = jnp.concatenate(outs, axis=1) * ng_ref[...]
        half_gate = 0.5 * gate_ref[0, r, :]
        o_ref[0, r, :] = (o * (half_gate + half_gate * jnp.tanh(half_gate))).astype(o_ref.dtype)


def _hgrn(hq, hf, hi, hgate, lb_logits, hg_norm_g, seq):
    batch = hq.shape[0]
    rows = HG_STEP_CHUNKS * HG_CHUNK
    assert FRONT == rows, "the first grid step must hold exactly the rows in front of x"
    in_spec = pl.BlockSpec((1, rows, HG_WIDTH), lambda b, j: (b, j, 0))
    zmat = _hgrn_exponent_matrix()
    masks = _hgrn_level_masks()
    return pl.pallas_call(
        _hgrn_kernel,
        grid=(batch, (FRONT + seq) // rows),
        in_specs=[in_spec, in_spec, in_spec, in_spec,
                  _const_spec(lb_logits.shape), _const_spec((1, HG_WIDTH)),
                  _const_spec(zmat.shape), _const_spec(masks.shape)],
        out_specs=pl.BlockSpec((1, rows, HG_WIDTH), lambda b, j: (b, jnp.maximum(j - 1, 0), 0)),
        out_shape=jax.ShapeDtypeStruct((batch, seq, HG_WIDTH), BF16),
        scratch_shapes=[pltpu.VMEM((HG_HEADS, HG_DIM, HG_DIM), F32)],
        compiler_params=pltpu.CompilerParams(
            dimension_semantics=("parallel", "arbitrary"), vmem_limit_bytes=VMEM_LIMIT),
        name="hgrn2",
    )(hq, hf, hi, hgate, lb_logits, hg_norm_g, zmat, masks)


def _ffn_kernel(x_ref, osb_ref, ohg_ref, wout_ref, g2_ref, wg_ref, wu_ref, wd_ref, gf_ref, o_ref):
    half = FFN_ROWS // 2
    width = D_FF // FFN_SPLIT
    h1, u2, ffn = {}, {}, {}

    def mix_and_norm(r):
        rows = slice(r * half, (r + 1) * half)
        mix = (jnp.dot(osb_ref[0, rows, :], wout_ref[0:SB_WIDTH, :], preferred_element_type=F32)
               + jnp.dot(ohg_ref[0, rows, :], wout_ref[SB_WIDTH:, :], preferred_element_type=F32))
        h1[r] = x_ref[0, rows, :] + mix
        u2[r] = (h1[r] * lax.rsqrt(jnp.mean(h1[r] * h1[r], axis=-1, keepdims=True) + RMS_EPS)
                 * g2_ref[...]).astype(BF16)

    def swiglu(r):
        acc = jnp.zeros_like(h1[r])
        for n in range(FFN_SPLIT):
            cols = slice(n * width, (n + 1) * width)
            gate = jnp.dot(u2[r], wg_ref[:, cols], preferred_element_type=F32)
            up = jnp.dot(u2[r], wu_ref[:, cols], preferred_element_type=F32)
            half_gate = 0.5 * gate
            act = ((half_gate + half_gate * jnp.tanh(half_gate)) * up).astype(BF16)
            acc = acc + jnp.dot(act, wd_ref[cols, :], preferred_element_type=F32)
        ffn[r] = acc

    def final_norm(r):
        rows = slice(r * half, (r + 1) * half)
        h2 = h1[r] + ffn[r]
        o_ref[0, rows, :] = (h2 * lax.rsqrt(jnp.mean(h2 * h2, axis=-1, keepdims=True) + RMS_EPS)
                             * gf_ref[...])

    phases = (mix_and_norm, swiglu, final_norm)
    for k in range(2 + len(phases) - 1):
        for lag, phase in enumerate(phases):
            if 0 <= k - lag < 2:
                phase(k - lag)


def _out_projection_ffn(x, o_sb, o_hg, w_out, norm2_g, w_gate, w_up, w_down, final_g):
    batch, seq, _ = x.shape
    row_spec = lambda width: pl.BlockSpec((1, FFN_ROWS, width), lambda b, j: (b, j, 0))
    resident = lambda shape: pl.BlockSpec(shape, lambda b, j: (0, 0), pipeline_mode=pl.Buffered(1))
    return pl.pallas_call(
        _ffn_kernel,
        grid=(batch, seq // FFN_ROWS),
        in_specs=[row_spec(D_MODEL), row_spec(SB_WIDTH), row_spec(HG_WIDTH),
                  resident((D_MODEL, D_MODEL)), resident((1, D_MODEL)),
                  resident((D_MODEL, D_FF)), resident((D_MODEL, D_FF)), resident((D_FF, D_MODEL)),
                  resident((1, D_MODEL))],
        out_specs=row_spec(D_MODEL),
        out_shape=jax.ShapeDtypeStruct((batch, seq, D_MODEL), x.dtype),
        compiler_params=pltpu.CompilerParams(
            dimension_semantics=("parallel", "parallel"), vmem_limit_bytes=VMEM_LIMIT),
        name="out_projection_ffn",
    )(x, o_sb, o_hg, w_out, norm2_g, w_gate, w_up, w_down, final_g)


def kernel(x, meta_tokens, norm1_g, w_in, sb_norm_g, hg_norm_g, hg_lb_logits, w_out, norm2_g,
           w_gate, w_up, w_down, final_norm_g):
    batch, seq, d_model = x.shape
    assert d_model == D_MODEL and seq % FFN_ROWS == 0 and seq % ATT_TILE == 0
    assert norm1_g.shape[0] == 1, "single-layer block"
    head = jnp.concatenate(
        [jnp.zeros((FIRST_VALID, D_MODEL), x.dtype), meta_tokens.astype(x.dtype)], axis=0)
    row = lambda g: g.reshape(1, -1).astype(F32)

    qkv, hq, hf, hi, hgate = _input_projection(x, head, row(norm1_g[0]), w_in[0].astype(F32))
    o_sb = _attention(qkv, row(sb_norm_g[0]), seq)
    o_hg = _hgrn(hq, hf, hi, hgate, hg_lb_logits.astype(F32), row(hg_norm_g[0]), seq)
    return _out_projection_ffn(
        x, o_sb, o_hg, w_out[0].astype(BF16), row(norm2_g[0]), w_gate[0].astype(BF16),
        w_up[0].astype(BF16), w_down[0].astype(BF16), row(final_norm_g))
```
